```python
import jax, jax.numpy as jnp
from jax import lax
import numpy as np

D_MODEL = 1024
BATCH = 16
SEQ = 2048
DEPTH = 1
DEC_BATCH = 32
DEC_SEQ = 32
PAST_LEN = 1024

CHUNK = 64
D_MIX = D_MODEL
D_CONV = D_MIX // 2
D_ATT = D_MIX - D_CONV
CONV_WIDTH = 31
N_HEADS = 8
QK_NOPE = 64
QK_ROPE = 32
V_DIM = D_ATT // N_HEADS
Q_LORA = 256
KV_LORA = 128
ROPE_THETA = 10000.0
EPS = 1e-6
Q_BLOCK = 128
ATT_SCALE = (QK_NOPE + QK_ROPE) ** -0.5
NEG = -1e30
SPLITS = list(np.cumsum([D_CONV, D_CONV, D_CONV, Q_LORA, KV_LORA, QK_ROPE])[:].tolist())
D_IN = 3 * D_CONV + Q_LORA + KV_LORA + QK_ROPE + D_ATT

kernel_name = "hymba_conformer_mla_stream_step"


def _rmsnorm(x, g):
    xf = x.astype(jnp.float32)
    y = xf * lax.rsqrt(jnp.mean(xf * xf, axis=-1, keepdims=True) + EPS)
    return (y * g.astype(jnp.float32)).astype(x.dtype)


def _layernorm(x, g, b):
    xf = x.astype(jnp.float32)
    mu = jnp.mean(xf, axis=-1, keepdims=True)
    var = jnp.mean(jnp.square(xf - mu), axis=-1, keepdims=True)
    y = (xf - mu) * lax.rsqrt(var + EPS) * g.astype(jnp.float32) + b.astype(jnp.float32)
    return y.astype(x.dtype)


def _rope_tables(pos):
    inv = ROPE_THETA ** (-jnp.arange(0, QK_ROPE, 2, dtype=jnp.float32) / QK_ROPE)
    ang = pos.astype(jnp.float32)[:, None] * inv[None, :]
    return jnp.cos(ang), jnp.sin(ang)


def _apply_rope(x, cos, sin):
    x1, x2 = jnp.split(x.astype(jnp.float32), 2, axis=-1)
    return jnp.concatenate([x1 * cos - x2 * sin, x2 * cos + x1 * sin], axis=-1).astype(x.dtype)


def _scores(q_nope, q_rope, k_nope, k_rope):
    s = jnp.einsum('bqhd,bkhd->bhqk', q_nope, k_nope, preferred_element_type=jnp.float32)
    s = s + jnp.einsum('bqhr,bkr->bhqk', q_rope, k_rope, preferred_element_type=jnp.float32)
    return s * ATT_SCALE


def _attend_prompt(q_nope, q_rope, k_nope, k_rope, v):
    B, S = q_nope.shape[:2]
    nb = S // Q_BLOCK

    def blocks(t):
        return jnp.moveaxis(t.reshape((B, nb, Q_BLOCK) + t.shape[2:]), 1, 0)

    k_chunk = jnp.arange(S) // CHUNK

    def one(args):
        qn, qr, i = args
        q_chunk = (i * Q_BLOCK + jnp.arange(Q_BLOCK)) // CHUNK
        mask = k_chunk[None, :] <= q_chunk[:, None]
        s = jnp.where(mask[None, None], _scores(qn, qr, k_nope, k_rope), NEG)
        p = jax.nn.softmax(s, axis=-1).astype(v.dtype)
        return jnp.einsum('bhqk,bkhd->bqhd', p, v)

    o = lax.map(one, (blocks(q_nope), blocks(q_rope), jnp.arange(nb)))
    return jnp.moveaxis(o, 0, 1).reshape(B, S, N_HEADS * V_DIM)


def _attend_sample(q_nope, q_rope, k_nope, k_rope, v):
    B, T = q_nope.shape[:2]
    p = jax.nn.softmax(_scores(q_nope, q_rope, k_nope, k_rope), axis=-1).astype(v.dtype)
    return jnp.einsum('bhqk,bkhd->bqhd', p, v).reshape(B, T, N_HEADS * V_DIM)


def _conv_branch(glu, hist, conv_w, conv_b, ln_g, ln_b):
    xin = jnp.concatenate([hist, glu], axis=1)
    y = lax.conv_general_dilated(
        xin, conv_w[:, None, :], window_strides=(1,), padding='VALID',
        dimension_numbers=('NWC', 'WIO', 'NWC'), feature_group_count=D_CONV) + conv_b
    y = jax.nn.silu(_layernorm(y, ln_g, ln_b))
    return y, xin[:, -(CONV_WIDTH - 1):]


def _mixer_layer(x, pos, conv_hist, ckv_past, krope_past, g_pre, w_in, conv_w, conv_b,
                 conv_ln_g, conv_ln_b, g_qa, w_qb, g_kva, w_kvb, w_out, g_post):
    B, T = x.shape[:2]
    prompt = ckv_past is None
    h = _rmsnorm(x, g_pre)
    u = h @ w_in
    a, b, gate_c, q_c, kv_c, k_r, gate_a = jnp.split(u, SPLITS, axis=-1)

    glu = a * jax.nn.sigmoid(b)
    if conv_hist is None:
        conv_hist = jnp.zeros((B, CONV_WIDTH - 1, D_CONV), dtype=glu.dtype)
    y_conv, new_hist = _conv_branch(glu, conv_hist, conv_w, conv_b, conv_ln_g, conv_ln_b)

    cos, sin = _rope_tables(pos)
    q = (_rmsnorm(q_c, g_qa) @ w_qb).reshape(B, T, N_HEADS, QK_NOPE + QK_ROPE)
    q_nope = q[..., :QK_NOPE]
    q_rope = _apply_rope(q[..., QK_NOPE:], cos[None, :, None, :], sin[None, :, None, :])
    ckv = _rmsnorm(kv_c, g_kva)
    krope = _apply_rope(k_r, cos[None], sin[None])
    if prompt:
        ckv_all, krope_all = ckv, krope
    else:
        ckv_all = jnp.concatenate([ckv_past, ckv], axis=1)
        krope_all = jnp.concatenate([krope_past, krope], axis=1)
    L = ckv_all.shape[1]
    kv = (ckv_all @ w_kvb).reshape(B, L, N_HEADS, QK_NOPE + V_DIM)
    k_nope, v = kv[..., :QK_NOPE], kv[..., QK_NOPE:]
    if prompt:
        y_att = _attend_prompt(q_nope, q_rope, k_nope, krope_all, v)
    else:
        y_att = _attend_sample(q_nope, q_rope, k_nope, krope_all, v)

    mixed = jnp.concatenate([y_conv * jax.nn.silu(gate_c), y_att * jax.nn.silu(gate_a)], axis=-1) @ w_out
    return x + _rmsnorm(mixed, g_post), ckv, krope, new_hist


def setup_inputs(seed: int = 0) -> dict:
    key = jax.random.key(seed)
    ks = jax.random.split(key, 24)
    f32 = jnp.float32

    def nrm(k, shape, scale):
        return jax.random.normal(k, shape, f32) * scale

    def gain(k, n):
        return 1.0 + 0.01 * jax.random.normal(k, (DEPTH, n), f32)

    return {
        "x_prompt": nrm(ks[0], (BATCH, SEQ, D_MODEL), 1.0),
        "x_sample": nrm(ks[1], (DEC_BATCH, DEC_SEQ, D_MODEL), 1.0),
        "cache_ckv": nrm(ks[2], (DEPTH, DEC_BATCH, PAST_LEN, KV_LORA), 1.0),
        "cache_krope": nrm(ks[3], (DEPTH, DEC_BATCH, PAST_LEN, QK_ROPE), 1.0),
        "state_conv": nrm(ks[4], (DEPTH, DEC_BATCH, CONV_WIDTH - 1, D_CONV), 0.5),
        "g_pre": gain(ks[5], D_MODEL),
        "w_in": nrm(ks[6], (DEPTH, D_MODEL, D_IN), D_MODEL ** -0.5),
        "conv_w": nrm(ks[7], (DEPTH, CONV_WIDTH, D_CONV), CONV_WIDTH ** -0.5),
        "conv_b": nrm(ks[8], (DEPTH, D_CONV), 0.01),
        "conv_ln_g": gain(ks[9], D_CONV),
        "conv_ln_b": nrm(ks[10], (DEPTH, D_CONV), 0.01),
        "g_qa": gain(ks[11], Q_LORA),
        "w_qb": nrm(ks[12], (DEPTH, Q_LORA, N_HEADS * (QK_NOPE + QK_ROPE)), Q_LORA ** -0.5),
        "g_kva": gain(ks[13], KV_LORA),
        "w_kvb": nrm(ks[14], (DEPTH, KV_LORA, N_HEADS * (QK_NOPE + V_DIM)), KV_LORA ** -0.5),
        "w_out": nrm(ks[15], (DEPTH, D_MIX, D_MODEL), D_MIX ** -0.5),
        "g_post": gain(ks[16], D_MODEL),
    }


def reference(x_prompt, x_sample, cache_ckv, cache_krope, state_conv, g_pre, w_in, conv_w, conv_b,
              conv_ln_g, conv_ln_b, g_qa, w_qb, g_kva, w_kvb, w_out, g_post):
    pos_p = jnp.arange(x_prompt.shape[1])
    pos_s = PAST_LEN + jnp.arange(x_sample.shape[1])
    yp, ys = x_prompt, x_sample
    ckv_p, kr_p, cv_p, ckv_s, kr_s, cv_s = [], [], [], [], [], []
    for l in range(DEPTH):
        w = (g_pre[l], w_in[l], conv_w[l], conv_b[l], conv_ln_g[l], conv_ln_b[l],
             g_qa[l], w_qb[l], g_kva[l], w_kvb[l], w_out[l], g_post[l])
        yp, c1, r1, h1 = _mixer_layer(yp, pos_p, None, None, None, *w)
        ys, c2, r2, h2 = _mixer_layer(ys, pos_s, state_conv[l], cache_ckv[l], cache_krope[l], *w)
        ckv_p.append(c1); kr_p.append(r1); cv_p.append(h1)
        ckv_s.append(c2); kr_s.append(r2); cv_s.append(h2)
    new_ckv_prompt = jnp.stack(ckv_p)
    new_krope_prompt = jnp.stack(kr_p)
    new_conv_prompt = jnp.stack(cv_p)
    new_ckv_sample = jnp.stack(ckv_s)
    new_krope_sample = jnp.stack(kr_s)
    new_conv_sample = jnp.stack(cv_s)
    return (yp, ys, new_ckv_prompt, new_krope_prompt, new_conv_prompt,
            new_ckv_sample, new_krope_sample, new_conv_sample)
```

```python
import functools

import jax
import jax.numpy as jnp
import numpy as np
from jax import lax
from jax.experimental import pallas as pl
from jax.experimental.pallas import tpu as pltpu

D_MODEL = 1024
CHUNK = 64
D_CONV = 512
D_ATT = 512
CONV_WIDTH = 31
HIST = CONV_WIDTH - 1
N_HEADS = 8
QK_NOPE = 64
QK_ROPE = 32
V_DIM = 64
Q_LORA = 256
KV_LORA = 128
ROPE_THETA = 10000.0
EPS = 1e-6
ATT_SCALE = (QK_NOPE + QK_ROPE) ** -0.5
NEG = -1e30

LANES = 128
HIST_PAD = 32
HEAD_W = N_HEADS * LANES
ROPE_LO, NOPE_LO, NOPE_HI = 0, QK_ROPE, QK_ROPE + QK_NOPE
C_A, C_B, C_GC, C_GA, C_QC, C_KVC, C_KR, C_END = 0, 512, 1024, 1536, 2048, 2304, 2432, 2688

PROJ_ROWS = 512
CONV_ROWS = 64
ATT_TQ = 256
OUT_ROWS = 512
SAMPLE_GROUP = 8
VMEM_LIMIT = 56 * 1024 * 1024

f32 = jnp.float32
bf16 = jnp.bfloat16


def _sigmoid(x):
    return 1.0 / (1.0 + jnp.exp(-x))


def _rms(x, g):
    return x * lax.rsqrt(jnp.mean(x * x, axis=-1, keepdims=True) + EPS) * g


def _proj_body(nb, seg, x_ref, hist_ref, cs_ref, sn_ref, gpre_ref, wmain_ref, convw_ref, convb_ref,
               lng_ref, lnb_ref, gqa_ref, wq_ref, gkva_ref, wkv_ref, eone_ref,
               q_ref, k_ref, v_ref, yc_ref, ga_ref, ckv_ref, kr_ref, nh_ref, gbuf, ybuf):
    rows = nb * seg
    rc = min(CONV_ROWS, seg)
    si = pl.program_id(1)

    x = x_ref[...].reshape(rows, D_MODEL)
    h = _rms(x, gpre_ref[...]).astype(bf16)

    def proj(c0, c1):
        return jnp.dot(h, wmain_ref[:, c0:c1], preferred_element_type=f32)

    ab = proj(C_A, C_GC)
    glu = ab[:, :D_CONV] * _sigmoid(ab[:, D_CONV:])

    @pl.when(si == 0)
    def _():
        gbuf[:, 0:HIST_PAD, :] = hist_ref[...]

    gbuf[:, HIST_PAD:HIST_PAD + seg, :] = glu.reshape(nb, seg, D_CONV)

    for j in range(nb):
        for i in range(seg // rc):
            r0 = i * rc
            cols = []
            for c in range(D_CONV // LANES):
                lanes = slice(c * LANES, (c + 1) * LANES)
                acc = jnp.broadcast_to(convb_ref[:, lanes], (rc, LANES))
                for t in range(CONV_WIDTH):
                    acc = acc + gbuf[j, pl.ds(r0 + t + HIST_PAD - HIST, rc), lanes] * convw_ref[t:t + 1, lanes]
                cols.append(acc)
            y = jnp.concatenate(cols, axis=1)
            mu = jnp.mean(y, axis=-1, keepdims=True)
            d = y - mu
            var = jnp.mean(d * d, axis=-1, keepdims=True)
            yn = d * lax.rsqrt(var + EPS) * lng_ref[...] + lnb_ref[...]
            ybuf[j, pl.ds(r0, rc), :] = yn * _sigmoid(yn)

    tail = gbuf[:, seg:seg + HIST_PAD, :]
    nh_ref[...] = tail
    gbuf[:, 0:HIST_PAD, :] = tail

    gc = proj(C_GC, C_GA)
    yc_ref[...] = (ybuf[...] * (gc * _sigmoid(gc)).reshape(nb, seg, D_CONV)).astype(bf16)
    ga = proj(C_GA, C_QC)
    ga_ref[...] = (ga * _sigmoid(ga)).astype(bf16).reshape(nb, seg, D_ATT)

    cs = cs_ref[...]
    sn = sn_ref[...]

    qn = _rms(proj(C_QC, C_KVC), gqa_ref[...]).astype(bf16)
    qq = jnp.dot(qn, wq_ref[...], preferred_element_type=f32)
    for hh in range(N_HEADS):
        lo = hh * LANES
        qa = qq[:, lo:lo + LANES].reshape(nb, seg, LANES)
        qb = qq[:, HEAD_W + lo:HEAD_W + lo + LANES].reshape(nb, seg, LANES)
        q_ref[:, :, lo:lo + LANES] = ((qa * cs + qb * sn) * ATT_SCALE).astype(bf16)

    ckv = _rms(proj(C_KVC, C_KR), gkva_ref[...])
    ckv_ref[...] = ckv.reshape(nb, seg, KV_LORA)
    krr = proj(C_KR, C_END)
    krope = krr[:, :LANES].reshape(nb, seg, LANES) * cs + krr[:, LANES:].reshape(nb, seg, LANES) * sn
    kr_ref[...] = krope[:, :, :QK_ROPE]
    kk = jnp.dot(ckv.astype(bf16), wkv_ref[...], preferred_element_type=f32)
    for hh in range(N_HEADS):
        lo = hh * LANES
        k_ref[:, :, lo:lo + LANES] = (kk[:, lo:lo + LANES].reshape(nb, seg, LANES) + krope).astype(bf16)
        v_ref[:, :, lo:lo + LANES] = (kk[:, HEAD_W + lo:HEAD_W + lo + LANES]
                                      + eone_ref[:, lo:lo + LANES]).astype(bf16).reshape(nb, seg, LANES)


def _proj_call(x, hist, cs, sn, w, nb, seg):
    B, T, _ = x.shape
    grid = (B // nb, T // seg)
    full = lambda a: pl.BlockSpec(a.shape, lambda b, s: (0,) * a.ndim)
    tok = lambda width: pl.BlockSpec((nb, seg, width), lambda b, s: (b, s, 0))
    weights = (w["g_pre"], w["w_main"], w["conv_w"], w["conv_b"], w["ln_g"], w["ln_b"],
               w["g_qa"], w["w_q"], w["g_kva"], w["w_kv"], w["e_one"])
    out_shape = (
        jax.ShapeDtypeStruct((B, T, HEAD_W), bf16),
        jax.ShapeDtypeStruct((B, T, HEAD_W), bf16),
        jax.ShapeDtypeStruct((B, T, HEAD_W), bf16),
        jax.ShapeDtypeStruct((B, T, D_CONV), bf16),
        jax.ShapeDtypeStruct((B, T, D_ATT), bf16),
        jax.ShapeDtypeStruct((B, T, KV_LORA), f32),
        jax.ShapeDtypeStruct((B, T, QK_ROPE), f32),
        jax.ShapeDtypeStruct((B, HIST_PAD, D_CONV), f32),
    )
    out_specs = (tok(HEAD_W), tok(HEAD_W), tok(HEAD_W), tok(D_CONV), tok(D_ATT), tok(KV_LORA), tok(QK_ROPE),
                 pl.BlockSpec((nb, HIST_PAD, D_CONV), lambda b, s: (b, 0, 0)))
    return pl.pallas_call(
        functools.partial(_proj_body, nb, seg),
        grid=grid,
        in_specs=[tok(D_MODEL),
                  pl.BlockSpec((nb, HIST_PAD, D_CONV), lambda b, s: (b, 0, 0)),
                  pl.BlockSpec((seg, LANES), lambda b, s: (s, 0)),
                  pl.BlockSpec((seg, LANES), lambda b, s: (s, 0))] + [full(a) for a in weights],
        out_specs=out_specs,
        out_shape=out_shape,
        scratch_shapes=[pltpu.VMEM((nb, seg + HIST_PAD, D_CONV), f32),
                        pltpu.VMEM((nb, seg, D_CONV), f32)],
        compiler_params=pltpu.CompilerParams(dimension_semantics=("arbitrary", "arbitrary"),
                                             vmem_limit_bytes=VMEM_LIMIT),
        name="proj",
    )(x, hist, cs, sn, *weights)


def _softmax_pv(parts):
    m = None
    for s, _ in parts:
        mi = jnp.max(s, axis=-1, keepdims=True)
        m = mi if m is None else jnp.maximum(m, mi)
    o = None
    for s, v in parts:
        oi = jnp.dot(jnp.exp(s - m).astype(bf16), v, preferred_element_type=f32)
        o = oi if o is None else o + oi
    return o


def _attend_prompt_body(seq, q_ref, k_ref, v_ref, ga_ref, o_ref):
    tq = ATT_TQ
    row_chunk = lax.broadcasted_iota(jnp.int32, (tq, tq), 0) // CHUNK
    col_chunk = lax.broadcasted_iota(jnp.int32, (tq, tq), 1) // CHUNK
    visible = col_chunk <= row_chunk
    lane = lax.broadcasted_iota(jnp.int32, (tq, LANES), 1)
    nt = (((1,), (1,)), ((), ()))
    for qi in range(seq // tq):
        q0 = qi * tq
        heads = []
        for e in range(2):
            lanes = slice(e * LANES, (e + 1) * LANES)
            qh = q_ref[0, q0:q0 + tq, lanes]
            s_diag = lax.dot_general(qh, k_ref[0, q0:q0 + tq, lanes], nt, preferred_element_type=f32)
            parts = [(jnp.where(visible, s_diag, NEG), v_ref[0, q0:q0 + tq, lanes])]
            if qi > 0:
                s_off = lax.dot_general(qh, k_ref[0, 0:q0, lanes], nt, preferred_element_type=f32)
                parts.append((s_off, v_ref[0, 0:q0, lanes]))
            o = _softmax_pv(parts)
            sum_lane = V_DIM if e == 0 else 0
            heads.append(o / o[:, sum_lane:sum_lane + 1])
        y = jnp.where(lane < V_DIM, heads[0], heads[1])
        o_ref[0, q0:q0 + tq, :] = (y * ga_ref[0, q0:q0 + tq, :].astype(f32)).astype(bf16)


def _attend_prompt_call(q, k, v, ga):
    B, S, _ = q.shape
    pair = pl.BlockSpec((1, S, 2 * LANES), lambda b, p: (b, 0, p))
    half = pl.BlockSpec((1, S, LANES), lambda b, p: (b, 0, p))
    return pl.pallas_call(
        functools.partial(_attend_prompt_body, S),
        grid=(B, N_HEADS // 2),
        in_specs=[pair, pair, pair, half],
        out_specs=half,
        out_shape=jax.ShapeDtypeStruct((B, S, D_ATT), bf16),
        compiler_params=pltpu.CompilerParams(dimension_semantics=("arbitrary", "arbitrary"),
                                             vmem_limit_bytes=VMEM_LIMIT),
        name="attend_prompt",
    )(q, k, v, ga)


def _attend_sample_body(q_ref, ckvn_ref, krn_ref, ckvp_ref, krp_ref, wkt_ref, wv_ref, ga_ref, o_ref):
    t = q_ref.shape[1]
    nt = (((1,), (1,)), ((), ()))
    q = q_ref[0]
    q_lat = jnp.concatenate(
        [jnp.dot(q[:, hh * LANES:(hh + 1) * LANES], wkt_ref[hh], preferred_element_type=f32)
         for hh in range(N_HEADS)], axis=0).astype(bf16)
    q_rope = jnp.concatenate(
        [q[:, hh * LANES + ROPE_LO:hh * LANES + ROPE_LO + QK_ROPE] for hh in range(N_HEADS)], axis=0)
    ckv_p = ckvp_ref[0].astype(bf16)
    ckv_n = ckvn_ref[0].astype(bf16)
    kr_p = krp_ref[0].astype(bf16)
    kr_n = krn_ref[0].astype(bf16)
    s_p = (lax.dot_general(q_lat, ckv_p, nt, preferred_element_type=f32)
           + lax.dot_general(q_rope, kr_p, nt, preferred_element_type=f32))
    s_n = (lax.dot_general(q_lat, ckv_n, nt, preferred_element_type=f32)
           + lax.dot_general(q_rope, kr_n, nt, preferred_element_type=f32))
    m = jnp.maximum(jnp.max(s_p, axis=-1, keepdims=True), jnp.max(s_n, axis=-1, keepdims=True))
    p_p = jnp.exp(s_p - m).astype(bf16)
    p_n = jnp.exp(s_n - m).astype(bf16)
    l = (jnp.sum(p_p.astype(f32), axis=-1, keepdims=True) + jnp.sum(p_n.astype(f32), axis=-1, keepdims=True))
    o_lat = (jnp.dot(p_p, ckv_p, preferred_element_type=f32)
             + jnp.dot(p_n, ckv_n, preferred_element_type=f32)) / l
    o_lat = o_lat.astype(bf16)
    y = None
    for hh in range(N_HEADS):
        yh = jnp.dot(o_lat[hh * t:(hh + 1) * t, :], wv_ref[hh], preferred_element_type=f32)
        y = yh if y is None else y + yh
    o_ref[0] = (y * ga_ref[0].astype(f32)).astype(bf16)


def _attend_sample_call(q, ckv_new, kr_new, ckv_past, kr_past, w, ga):
    B, T, _ = q.shape
    P = ckv_past.shape[1]
    per_b = lambda shape: pl.BlockSpec((1,) + shape, lambda b: (b, 0, 0))
    full = lambda a: pl.BlockSpec(a.shape, lambda b: (0,) * a.ndim)
    return pl.pallas_call(
        _attend_sample_body,
        grid=(B,),
        in_specs=[per_b((T, HEAD_W)), per_b((T, KV_LORA)), per_b((T, QK_ROPE)),
                  per_b((P, KV_LORA)), per_b((P, QK_ROPE)), full(w["w_kt"]), full(w["w_vh"]),
                  per_b((T, D_ATT))],
        out_specs=per_b((T, D_ATT)),
        out_shape=jax.ShapeDtypeStruct((B, T, D_ATT), bf16),
        compiler_params=pltpu.CompilerParams(dimension_semantics=("arbitrary",),
                                             vmem_limit_bytes=VMEM_LIMIT),
        name="attend_sample",
    )(q, ckv_new, kr_new, ckv_past, kr_past, w["w_kt"], w["w_vh"], ga)


def _out_body(yc_ref, ya_ref, x_ref, wout_ref, gpost_ref, y_ref):
    mixed = (jnp.dot(yc_ref[...], wout_ref[0:D_CONV, :], preferred_element_type=f32)
             + jnp.dot(ya_ref[...], wout_ref[D_CONV:, :], preferred_element_type=f32))
    y_ref[...] = x_ref[...] + _rms(mixed, gpost_ref[...])


def _out_call(yc, ya, x, w):
    n = x.shape[0]
    rows = min(OUT_ROWS, n)
    tok = lambda width: pl.BlockSpec((rows, width), lambda i: (i, 0))
    full = lambda a: pl.BlockSpec(a.shape, lambda i: (0,) * a.ndim)
    return pl.pallas_call(
        _out_body,
        grid=(n // rows,),
        in_specs=[tok(D_CONV), tok(D_ATT), tok(D_MODEL), full(w["w_out"]), full(w["g_post"])],
        out_specs=tok(D_MODEL),
        out_shape=jax.ShapeDtypeStruct((n, D_MODEL), f32),
        compiler_params=pltpu.CompilerParams(dimension_semantics=("arbitrary",),
                                             vmem_limit_bytes=VMEM_LIMIT),
        name="out",
    )(yc, ya, x, w["w_out"], w["g_post"])


def _rot_cols(w_rope):
    half = QK_ROPE // 2
    return jnp.concatenate([-w_rope[:, half:], w_rope[:, :half]], axis=1)


def _head_group(rope, nope):
    k = (rope if rope is not None else nope).shape[0]
    z = lambda n: jnp.zeros((k, n), f32)
    return jnp.concatenate([rope if rope is not None else z(QK_ROPE),
                            nope if nope is not None else z(QK_NOPE),
                            z(LANES - NOPE_HI)], axis=1)


def _layout_weights(g_pre, w_in, conv_w, conv_b, ln_g, ln_b, g_qa, w_qb, g_kva, w_kvb, w_out, g_post):
    a, b, gate_c, q_c, kv_c, k_r, gate_a = jnp.split(
        w_in, np.cumsum([D_CONV, D_CONV, D_CONV, Q_LORA, KV_LORA, QK_ROPE]).tolist(), axis=1)
    w_main = jnp.concatenate(
        [a, b, gate_c, gate_a, q_c, kv_c, _head_group(k_r, None), _head_group(_rot_cols(k_r), None)], axis=1)

    dq = QK_NOPE + QK_ROPE
    q_plain, q_rot = [], []
    for h in range(N_HEADS):
        nope = w_qb[:, h * dq:h * dq + QK_NOPE]
        rope = w_qb[:, h * dq + QK_NOPE:(h + 1) * dq]
        q_plain.append(_head_group(rope, nope))
        q_rot.append(_head_group(_rot_cols(rope), None))
    w_q = jnp.concatenate(q_plain + q_rot, axis=1)

    dkv = QK_NOPE + V_DIM
    k_cols, v_cols, e_one, w_kt, w_vh = [], [], [], [], []
    zeros_v = jnp.zeros((KV_LORA, LANES - V_DIM), f32)
    for h in range(N_HEADS):
        wk = w_kvb[:, h * dkv:h * dkv + QK_NOPE]
        wv = w_kvb[:, h * dkv + QK_NOPE:(h + 1) * dkv]
        k_cols.append(_head_group(None, wk))
        v_cols.append(jnp.concatenate([wv, zeros_v] if h % 2 == 0 else [zeros_v, wv], axis=1))
        e_one.append(jnp.zeros((1, LANES), f32).at[0, V_DIM if h % 2 == 0 else 0].set(1.0))
        w_kt.append(_head_group(None, wk).T)
        w_vh.append(jnp.zeros((KV_LORA, D_ATT), f32).at[:, h * V_DIM:(h + 1) * V_DIM].set(wv))
    w_kv = jnp.concatenate(k_cols + v_cols, axis=1)

    row = lambda v: v.reshape(1, -1)
    return {
        "g_pre": row(g_pre), "w_main": w_main.astype(bf16), "conv_w": conv_w, "conv_b": row(conv_b),
        "ln_g": row(ln_g), "ln_b": row(ln_b), "g_qa": row(g_qa), "w_q": w_q.astype(bf16),
        "g_kva": row(g_kva), "w_kv": w_kv.astype(bf16), "e_one": jnp.concatenate(e_one, axis=1),
        "w_kt": jnp.stack(w_kt).astype(bf16), "w_vh": jnp.stack(w_vh).astype(bf16),
        "w_out": w_out.astype(bf16), "g_post": row(g_post),
    }


def _rope_tables(pos):
    inv = ROPE_THETA ** (-jnp.arange(0, QK_ROPE, 2, dtype=f32) / QK_ROPE)
    ang = pos.astype(f32)[:, None] * inv[None, :]
    cos, sin = jnp.cos(ang), jnp.sin(ang)
    t = pos.shape[0]
    cs = jnp.concatenate([cos, cos, jnp.ones((t, QK_NOPE), f32), jnp.zeros((t, LANES - NOPE_HI), f32)], axis=1)
    sn = jnp.concatenate([sin, sin, jnp.zeros((t, LANES - QK_ROPE), f32)], axis=1)
    return cs, sn


def _layer(x_prompt, x_sample, ckv_past, kr_past, conv_state, w):
    B, S, _ = x_prompt.shape
    Bs, T, _ = x_sample.shape
    past = ckv_past.shape[1]
    pad_hist = lambda hst: jnp.pad(hst, ((0, 0), (HIST_PAD - HIST, 0), (0, 0)))

    cs_p, sn_p = _rope_tables(jnp.arange(S))
    qp, kp, vp, ycp, gap, ckvp, krp, nhp = _proj_call(
        x_prompt, jnp.zeros((B, HIST_PAD, D_CONV), f32), cs_p, sn_p, w, 1, PROJ_ROWS)
    yap = _attend_prompt_call(qp, kp, vp, gap)
    y_prompt = _out_call(ycp.reshape(B * S, D_CONV), yap.reshape(B * S, D_ATT),
                         x_prompt.reshape(B * S, D_MODEL), w).reshape(B, S, D_MODEL)

    cs_s, sn_s = _rope_tables(past + jnp.arange(T))
    qs, _, _, ycs, gas, ckvs, krs, nhs = _proj_call(x_sample, pad_hist(conv_state), cs_s, sn_s, w, SAMPLE_GROUP, T)
    yas = _attend_sample_call(qs, ckvs, krs, ckv_past, kr_past, w, gas)
    y_sample = _out_call(ycs.reshape(Bs * T, D_CONV), yas.reshape(Bs * T, D_ATT),
                         x_sample.reshape(Bs * T, D_MODEL), w).reshape(Bs, T, D_MODEL)

    trim = lambda nh: nh[:, HIST_PAD - HIST:, :]
    return y_prompt, y_sample, ckvp, krp, trim(nhp), ckvs, krs, trim(nhs)


def kernel(x_prompt, x_sample, cache_ckv, cache_krope, state_conv, g_pre, w_in, conv_w, conv_b, conv_ln_g,
           conv_ln_b, g_qa, w_qb, g_kva, w_kvb, w_out, g_post):
    depth = w_in.shape[0]
    yp, ys = x_prompt, x_sample
    outs = [[] for _ in range(6)]
    for l in range(depth):
        w = _layout_weights(g_pre[l], w_in[l], conv_w[l], conv_b[l], conv_ln_g[l], conv_ln_b[l], g_qa[l],
                            w_qb[l], g_kva[l], w_kvb[l], w_out[l], g_post[l])
        yp, ys, *caches = _layer(yp, ys, cache_ckv[l], cache_krope[l], state_conv[l], w)
        for dst, val in zip(outs, caches):
            dst.append(val)
    return (yp, ys) + tuple(jnp.stack(o) for o in outs)
```

```python
import functools

import jax
import jax.numpy as jnp
import numpy as np
from jax import lax
from jax.experimental import pallas as pl
from jax.experimental.pallas import tpu as pltpu

D_MODEL = 1024
CHUNK = 64
D_CONV = 512
D_ATT = 512
CONV_WIDTH = 31
HIST = CONV_WIDTH - 1
N_HEADS = 8
QK_NOPE = 64
QK_ROPE = 32
V_DIM = 64
Q_LORA = 256
KV_LORA = 128
ROPE_THETA = 10000.0
EPS = 1e-6
ATT_SCALE = (QK_NOPE + QK_ROPE) ** -0.5
NEG = -1e30

LANES = 128
SUBLANES = 8
HIST_PAD = 32
HEAD_W = N_HEADS * LANES
ROPE_LO = QK_NOPE
HALF = QK_ROPE // 2
R_A, R_GC, R_QC, R_KVC, R_KR, R_GA, R_END = 0, 1024, 1536, 1792, 1920, 1952, 2464

PROJ_ROWS = 512
CONV_ROWS = 64
ATT_TQ = 256
OUT_ROWS = 512
SAMPLE_GROUP = 8
VMEM_LIMIT = 56 * 1024 * 1024

f32 = jnp.float32
bf16 = jnp.bfloat16
NT = (((1,), (1,)), ((), ()))


def _sigmoid(x):
    return 1.0 / (1.0 + jnp.exp(-x))


def _rms(x, g):
    return x * lax.rsqrt(jnp.mean(x * x, axis=-1, keepdims=True) + EPS) * g


def _conv_chunk(gbuf, j, r0, rc, lanes, convw_ref, convb_ref):
    first = HIST_PAD - HIST
    ext = rc + SUBLANES
    y = jnp.broadcast_to(convb_ref[:, lanes], (rc, LANES))
    for s in range(SUBLANES):
        nrows = rc if s == 0 else ext
        part = None
        for o in range(s, first + CONV_WIDTH, SUBLANES):
            if o < first:
                continue
            term = gbuf[j, r0 + o - s:r0 + o - s + nrows, lanes] * convw_ref[o - first:o - first + 1, lanes]
            part = term if part is None else part + term
        y = y + (part if s == 0 else pltpu.roll(part, ext - s, axis=0)[:rc])
    return y


def _proj_body(nb, seg, kr_transposed, x_ref, hist_ref, csq_ref, snq_ref, csk_ref, snk_ref, gpre_ref, win_ref,
               convw_ref, convb_ref, lng_ref, lnb_ref, gqa_ref, wq_ref, gkva_ref, wkv_ref, eone_ref,
               q_ref, k_ref, v_ref, yc_ref, ga_ref, ckv_ref, kr_ref, nh_ref, gbuf, ybuf):
    rows = nb * seg
    rc = min(CONV_ROWS, seg)
    si = pl.program_id(1)

    x = x_ref[...].reshape(rows, D_MODEL)
    h = _rms(x, gpre_ref[...]).astype(bf16)

    def proj(r0, r1):
        return lax.dot_general(h, win_ref[r0:r1, :], NT, preferred_element_type=f32)

    ab = proj(R_A, R_GC)
    glu = ab[:, :D_CONV] * _sigmoid(ab[:, D_CONV:])

    @pl.when(si == 0)
    def _():
        gbuf[:, 0:HIST_PAD, :] = hist_ref[...]

    gbuf[:, HIST_PAD:HIST_PAD + seg, :] = glu.reshape(nb, seg, D_CONV)

    for j in range(nb):
        for i in range(seg // rc):
            r0 = i * rc
            cols = []
            for c in range(D_CONV // LANES):
                cols.append(_conv_chunk(gbuf, j, r0, rc, slice(c * LANES, (c + 1) * LANES), convw_ref, convb_ref))
            y = jnp.concatenate(cols, axis=1)
            mu = jnp.mean(y, axis=-1, keepdims=True)
            d = y - mu
            var = jnp.mean(d * d, axis=-1, keepdims=True)
            yn = d * lax.rsqrt(var + EPS) * lng_ref[...] + lnb_ref[...]
            ybuf[j, pl.ds(r0, rc), :] = yn * _sigmoid(yn)

    tail = gbuf[:, seg:seg + HIST_PAD, :]
    nh_ref[...] = tail
    gbuf[:, 0:HIST_PAD, :] = tail

    gc = proj(R_GC, R_QC)
    yc_ref[...] = (ybuf[...] * (gc * _sigmoid(gc)).reshape(nb, seg, D_CONV)).astype(bf16)
    ga = proj(R_GA, R_END)
    ga_ref[...] = (ga * _sigmoid(ga)).astype(bf16).reshape(nb, seg, D_ATT)

    t0 = pl.multiple_of(si * seg, seg)
    rows_of = lambda ref: ref[pl.ds(t0, seg), :]

    qn = _rms(proj(R_QC, R_KVC), gqa_ref[...]).astype(bf16)
    qq = jnp.dot(qn, wq_ref[...], preferred_element_type=f32)
    csq, snq = rows_of(csq_ref), rows_of(snq_ref)
    for hh in range(N_HEADS):
        lo = hh * LANES
        qa = qq[:, lo:lo + LANES].reshape(nb, seg, LANES)
        qb = qq[:, HEAD_W + lo:HEAD_W + lo + LANES].reshape(nb, seg, LANES)
        q_ref[:, :, lo:lo + LANES] = (qa * csq + qb * snq).astype(bf16)

    kr = proj(R_KR, R_KR + LANES)
    lane = lax.broadcasted_iota(jnp.int32, (rows, LANES), 1)
    kr_rot = jnp.where(lane < HALF, pltpu.roll(kr, LANES - HALF, axis=1), pltpu.roll(kr, HALF, axis=1))
    krope = (kr.reshape(nb, seg, LANES) * rows_of(csk_ref) + kr_rot.reshape(nb, seg, LANES) * rows_of(snk_ref))
    if kr_transposed:
        kr_ref[0] = krope[0].T[0:QK_ROPE, :]
    else:
        kr_ref[...] = krope[:, :, 0:QK_ROPE]
    krope_k = pltpu.roll(krope.reshape(rows, LANES), ROPE_LO, axis=1)

    ckv = _rms(proj(R_KVC, R_KR), gkva_ref[...])
    ckv_ref[...] = ckv.reshape(nb, seg, KV_LORA)
    kk = jnp.dot(ckv.astype(bf16), wkv_ref[...], preferred_element_type=f32)
    for hh in range(N_HEADS):
        lo = hh * LANES
        k_ref[:, :, lo:lo + LANES] = (kk[:, lo:lo + LANES] + krope_k).astype(bf16).reshape(nb, seg, LANES)
        v_ref[:, :, lo:lo + LANES] = (kk[:, HEAD_W + lo:HEAD_W + lo + LANES]
                                      + eone_ref[...]).astype(bf16).reshape(nb, seg, LANES)


def _proj_call(x, hist, tables, w, nb, seg, kr_transposed):
    B, T, _ = x.shape
    grid = (B // nb, T // seg)
    full = lambda a: pl.BlockSpec(a.shape, lambda b, s: (0,) * a.ndim)
    tok = lambda width: pl.BlockSpec((nb, seg, width), lambda b, s: (b, s, 0))
    weights = (w["g_pre"], w["w_in_t"], w["conv_w"], w["conv_b"], w["ln_g"], w["ln_b"],
               w["g_qa"], w["w_q"], w["g_kva"], w["w_kv"], w["e_one"])
    if kr_transposed:
        assert nb == 1
        kr_shape, kr_spec = (B, QK_ROPE, T), pl.BlockSpec((1, QK_ROPE, seg), lambda b, s: (b, 0, s))
    else:
        kr_shape, kr_spec = (B, T, QK_ROPE), tok(QK_ROPE)
    out_shape = (
        jax.ShapeDtypeStruct((B, T, HEAD_W), bf16),
        jax.ShapeDtypeStruct((B, T, HEAD_W), bf16),
        jax.ShapeDtypeStruct((B, T, HEAD_W), bf16),
        jax.ShapeDtypeStruct((B, T, D_CONV), bf16),
        jax.ShapeDtypeStruct((B, T, D_ATT), bf16),
        jax.ShapeDtypeStruct((B, T, KV_LORA), f32),
        jax.ShapeDtypeStruct(kr_shape, f32),
        jax.ShapeDtypeStruct((B, HIST_PAD, D_CONV), f32),
    )
    out_specs = (tok(HEAD_W), tok(HEAD_W), tok(HEAD_W), tok(D_CONV), tok(D_ATT), tok(KV_LORA), kr_spec,
                 pl.BlockSpec((nb, HIST_PAD, D_CONV), lambda b, s: (b, 0, 0)))
    return pl.pallas_call(
        functools.partial(_proj_body, nb, seg, kr_transposed),
        grid=grid,
        in_specs=[tok(D_MODEL), pl.BlockSpec((nb, HIST_PAD, D_CONV), lambda b, s: (b, 0, 0))]
        + [full(t) for t in tables] + [full(a) for a in weights],
        out_specs=out_specs,
        out_shape=out_shape,
        scratch_shapes=[pltpu.VMEM((nb, seg + HIST_PAD, D_CONV), f32),
                        pltpu.VMEM((nb, seg, D_CONV), f32)],
        compiler_params=pltpu.CompilerParams(dimension_semantics=("arbitrary", "arbitrary"),
                                             vmem_limit_bytes=VMEM_LIMIT),
        name="proj",
    )(x, hist, *tables, *weights)


def _softmax_pv(parts):
    m = None
    for s, _ in parts:
        mi = jnp.max(s, axis=-1, keepdims=True)
        m = mi if m is None else jnp.maximum(m, mi)
    o = None
    for s, v in parts:
        oi = jnp.dot(jnp.exp(s - m).astype(bf16), v, preferred_element_type=f32)
        o = oi if o is None else o + oi
    return o


def _attend_prompt_body(seq, q_ref, k_ref, v_ref, ga_ref, o_ref):
    tq = ATT_TQ
    row_chunk = lax.broadcasted_iota(jnp.int32, (tq, tq), 0) // CHUNK
    col_chunk = lax.broadcasted_iota(jnp.int32, (tq, tq), 1) // CHUNK
    visible = col_chunk <= row_chunk
    lane = lax.broadcasted_iota(jnp.int32, (tq, LANES), 1)
    for qi in range(seq // tq):
        q0 = qi * tq
        heads = []
        for e in range(2):
            lanes = slice(e * LANES, (e + 1) * LANES)
            qh = q_ref[0, q0:q0 + tq, lanes]
            s_diag = lax.dot_general(qh, k_ref[0, q0:q0 + tq, lanes], NT, preferred_element_type=f32)
            parts = [(jnp.where(visible, s_diag, NEG), v_ref[0, q0:q0 + tq, lanes])]
            if qi > 0:
                s_off = lax.dot_general(qh, k_ref[0, 0:q0, lanes], NT, preferred_element_type=f32)
                parts.append((s_off, v_ref[0, 0:q0, lanes]))
            o = _softmax_pv(parts)
            heads.append(o / o[:, 0:1])
        y = jnp.where(lane < V_DIM, pltpu.roll(heads[0], V_DIM, axis=1), heads[1])
        o_ref[0, q0:q0 + tq, :] = (y * ga_ref[0, q0:q0 + tq, :].astype(f32)).astype(bf16)


def _attend_prompt_call(q, k, v, ga):
    B, S, _ = q.shape
    pair = pl.BlockSpec((1, S, 2 * LANES), lambda b, p: (b, 0, p))
    half = pl.BlockSpec((1, S, LANES), lambda b, p: (b, 0, p))
    return pl.pallas_call(
        functools.partial(_attend_prompt_body, S),
        grid=(B, N_HEADS // 2),
        in_specs=[pair, pair, pair, half],
        out_specs=half,
        out_shape=jax.ShapeDtypeStruct((B, S, D_ATT), bf16),
        compiler_params=pltpu.CompilerParams(dimension_semantics=("arbitrary", "arbitrary"),
                                             vmem_limit_bytes=VMEM_LIMIT),
        name="attend_prompt",
    )(q, k, v, ga)


def _attend_sample_body(q_ref, ckvn_ref, krn_ref, ckvp_ref, krpt_ref, wabs_ref, wv_ref, ga_ref, o_ref):
    t = q_ref.shape[1]
    q = q_ref[0]
    q_cat = jnp.concatenate(
        [jnp.dot(q[:, hh * LANES:(hh + 1) * LANES], wabs_ref[hh], preferred_element_type=f32)
         for hh in range(N_HEADS)], axis=0).astype(bf16)
    q_lat = q_cat[:, :KV_LORA]
    q_rope = q_cat[:, KV_LORA:KV_LORA + QK_ROPE]
    ckv_p = ckvp_ref[0].astype(bf16)
    ckv_n = ckvn_ref[0].astype(bf16)
    s_p = (lax.dot_general(q_lat, ckv_p, NT, preferred_element_type=f32)
           + jnp.dot(q_rope, krpt_ref[0].astype(bf16), preferred_element_type=f32))
    s_n = (lax.dot_general(q_lat, ckv_n, NT, preferred_element_type=f32)
           + lax.dot_general(q_rope, krn_ref[0].astype(bf16), NT, preferred_element_type=f32))
    m = jnp.maximum(jnp.max(s_p, axis=-1, keepdims=True), jnp.max(s_n, axis=-1, keepdims=True))
    p_p = jnp.exp(s_p - m).astype(bf16)
    p_n = jnp.exp(s_n - m).astype(bf16)
    l = (jnp.sum(p_p.astype(f32), axis=-1, keepdims=True) + jnp.sum(p_n.astype(f32), axis=-1, keepdims=True))
    o_lat = (jnp.dot(p_p, ckv_p, preferred_element_type=f32)
             + jnp.dot(p_n, ckv_n, preferred_element_type=f32)) / l
    o_lat = o_lat.astype(bf16)
    y = None
    for hh in range(N_HEADS):
        yh = jnp.dot(o_lat[hh * t:(hh + 1) * t, :], wv_ref[hh], preferred_element_type=f32)
        y = yh if y is None else y + yh
    o_ref[0] = (y * ga_ref[0].astype(f32)).astype(bf16)


def _attend_sample_call(q, ckv_new, kr_new, ckv_past, kr_past_t, w, ga):
    B, T, _ = q.shape
    P = ckv_past.shape[1]
    per_b = lambda shape: pl.BlockSpec((1,) + shape, lambda b: (b, 0, 0))
    full = lambda a: pl.BlockSpec(a.shape, lambda b: (0,) * a.ndim)
    return pl.pallas_call(
        _attend_sample_body,
        grid=(B,),
        in_specs=[per_b((T, HEAD_W)), per_b((T, KV_LORA)), per_b((T, QK_ROPE)),
                  per_b((P, KV_LORA)), per_b((QK_ROPE, P)), full(w["w_abs"]), full(w["w_vh"]),
                  per_b((T, D_ATT))],
        out_specs=per_b((T, D_ATT)),
        out_shape=jax.ShapeDtypeStruct((B, T, D_ATT), bf16),
        compiler_params=pltpu.CompilerParams(dimension_semantics=("arbitrary",),
                                             vmem_limit_bytes=VMEM_LIMIT),
        name="attend_sample",
    )(q, ckv_new, kr_new, ckv_past, kr_past_t, w["w_abs"], w["w_vh"], ga)


def _out_body(yc_ref, ya_ref, x_ref, wout_ref, gpost_ref, y_ref):
    mixed = (jnp.dot(yc_ref[...], wout_ref[0:D_CONV, :], preferred_element_type=f32)
             + jnp.dot(ya_ref[...], wout_ref[D_CONV:, :], preferred_element_type=f32))
    y_ref[...] = x_ref[...] + _rms(mixed, gpost_ref[...])


def _out_call(yc, ya, x, w):
    n = x.shape[0]
    rows = min(OUT_ROWS, n)
    tok = lambda width: pl.BlockSpec((rows, width), lambda i: (i, 0))
    full = lambda a: pl.BlockSpec(a.shape, lambda i: (0,) * a.ndim)
    return pl.pallas_call(
        _out_body,
        grid=(n // rows,),
        in_specs=[tok(D_CONV), tok(D_ATT), tok(D_MODEL), full(w["w_out"]), full(w["g_post"])],
        out_specs=tok(D_MODEL),
        out_shape=jax.ShapeDtypeStruct((n, D_MODEL), f32),
        compiler_params=pltpu.CompilerParams(dimension_semantics=("arbitrary",),
                                             vmem_limit_bytes=VMEM_LIMIT),
        name="out",
    )(yc, ya, x, w["w_out"], w["g_post"])


def _rot_cols(rope):
    return jnp.concatenate([-rope[..., HALF:], rope[..., :HALF]], axis=-1)


def _layout_weights(g_pre, w_in, conv_w, conv_b, ln_g, ln_b, g_qa, w_qb, g_kva, w_kvb, w_out, g_post):
    dq = QK_NOPE + QK_ROPE
    pad_q = ((0, 0), (0, 0), (0, LANES - dq))
    wq3 = w_qb.reshape(Q_LORA, N_HEADS, dq)
    w_q_plain = jnp.pad(wq3, pad_q).reshape(Q_LORA, HEAD_W)
    w_q_rot = jnp.pad(_rot_cols(wq3[:, :, QK_NOPE:]), ((0, 0), (0, 0), (ROPE_LO, LANES - dq))).reshape(Q_LORA, HEAD_W)
    w_q = jnp.concatenate([w_q_plain, w_q_rot], axis=1)

    lane = jnp.arange(HEAD_W) % LANES
    w_kv = jnp.concatenate([jnp.where(lane < QK_NOPE, w_kvb, 0.0), jnp.where(lane >= QK_NOPE, w_kvb, 0.0)], axis=1)
    e_one = (jnp.arange(LANES) == 0).astype(f32)[None]

    wkv3 = w_kvb.reshape(KV_LORA, N_HEADS, QK_NOPE + V_DIM)
    wk_t = jnp.transpose(wkv3[:, :, :QK_NOPE], (1, 2, 0))
    wk_t = jnp.pad(wk_t, ((0, 0), (0, LANES - QK_NOPE), (0, 0)))
    sel = np.zeros((LANES, LANES), np.float32)
    sel[ROPE_LO + np.arange(QK_ROPE), np.arange(QK_ROPE)] = 1.0
    w_abs = jnp.concatenate([wk_t, jnp.broadcast_to(jnp.asarray(sel), (N_HEADS, LANES, LANES))], axis=2)
    wv = jnp.transpose(wkv3[:, :, QK_NOPE:], (1, 0, 2))
    w_vh = (wv[:, :, None, :] * jnp.eye(N_HEADS, dtype=f32)[:, None, :, None]).reshape(N_HEADS, KV_LORA, D_ATT)

    row = lambda v: v.reshape(1, -1)
    return {
        "g_pre": row(g_pre), "w_in_t": jnp.swapaxes(w_in, 0, 1).astype(bf16), "conv_w": conv_w,
        "conv_b": row(conv_b), "ln_g": row(ln_g), "ln_b": row(ln_b), "g_qa": row(g_qa), "w_q": w_q.astype(bf16),
        "g_kva": row(g_kva), "w_kv": w_kv.astype(bf16), "e_one": e_one,
        "w_abs": w_abs.astype(bf16), "w_vh": w_vh.astype(bf16),
        "w_out": w_out.astype(bf16), "g_post": row(g_post),
    }


def _rope_tables(pos):
    inv = ROPE_THETA ** (-jnp.arange(0, QK_ROPE, 2, dtype=f32) / QK_ROPE)
    ang = pos.astype(f32)[:, None] * inv[None, :]
    cos, sin = jnp.cos(ang), jnp.sin(ang)
    t = pos.shape[0]
    z = lambda n: jnp.zeros((t, n), f32)
    tail = LANES - QK_NOPE - QK_ROPE
    cs_q = jnp.concatenate([jnp.full((t, QK_NOPE), ATT_SCALE, f32), cos * ATT_SCALE, cos * ATT_SCALE, z(tail)], axis=1)
    sn_q = jnp.concatenate([z(QK_NOPE), sin * ATT_SCALE, sin * ATT_SCALE, z(tail)], axis=1)
    cs_k = jnp.concatenate([cos, cos, z(LANES - QK_ROPE)], axis=1)
    sn_k = jnp.concatenate([-sin, sin, z(LANES - QK_ROPE)], axis=1)
    return cs_q, sn_q, cs_k, sn_k


def _layer(x_prompt, x_sample, ckv_past, kr_past, conv_state, w):
    B, S, _ = x_prompt.shape
    Bs, T, _ = x_sample.shape
    past = ckv_past.shape[1]
    pad_hist = lambda hst: jnp.pad(hst, ((0, 0), (HIST_PAD - HIST, 0), (0, 0)))

    qp, kp, vp, ycp, gap, ckvp, krp_t, nhp = _proj_call(
        x_prompt, jnp.zeros((B, HIST_PAD, D_CONV), f32), _rope_tables(jnp.arange(S)), w, 1, PROJ_ROWS, True)
    yap = _attend_prompt_call(qp, kp, vp, gap)
    y_prompt = _out_call(ycp.reshape(B * S, D_CONV), yap.reshape(B * S, D_ATT),
                         x_prompt.reshape(B * S, D_MODEL), w).reshape(B, S, D_MODEL)

    qs, _, _, ycs, gas, ckvs, krs, nhs = _proj_call(
        x_sample, pad_hist(conv_state), _rope_tables(past + jnp.arange(T)), w, SAMPLE_GROUP, T, False)
    yas = _attend_sample_call(qs, ckvs, krs, ckv_past, jnp.swapaxes(kr_past, 1, 2), w, gas)
    y_sample = _out_call(ycs.reshape(Bs * T, D_CONV), yas.reshape(Bs * T, D_ATT),
                         x_sample.reshape(Bs * T, D_MODEL), w).reshape(Bs, T, D_MODEL)

    trim = lambda nh: nh[:, HIST_PAD - HIST:, :]
    return y_prompt, y_sample, ckvp, jnp.swapaxes(krp_t, 1, 2), trim(nhp), ckvs, krs, trim(nhs)


def kernel(x_prompt, x_sample, cache_ckv, cache_krope, state_conv, g_pre, w_in, conv_w, conv_b, conv_ln_g,
           conv_ln_b, g_qa, w_qb, g_kva, w_kvb, w_out, g_post):
    depth = w_in.shape[0]
    yp, ys = x_prompt, x_sample
    outs = [[] for _ in range(6)]
    for l in range(depth):
        w = _layout_weights(g_pre[l], w_in[l], conv_w[l], conv_b[l], conv_ln_g[l], conv_ln_b[l], g_qa[l],
                            w_qb[l], g_kva[l], w_kvb[l], w_out[l], g_post[l])
        yp, ys, *caches = _layer(yp, ys, cache_ckv[l], cache_krope[l], state_conv[l], w)
        for dst, val in zip(outs, caches):
            dst.append(val)
    return (yp, ys) + tuple(jnp.stack(o) for o in outs)
```

```python
import functools

import jax
import jax.numpy as jnp
import numpy as np
from jax import lax
from jax.experimental import pallas as pl
from jax.experimental.pallas import tpu as pltpu

D_MODEL = 1024
CHUNK = 64
D_CONV = 512
D_ATT = 512
CONV_WIDTH = 31
HIST = CONV_WIDTH - 1
N_HEADS = 8
QK_NOPE = 64
QK_ROPE = 32
V_DIM = 64
Q_LORA = 256
KV_LORA = 128
ROPE_THETA = 10000.0
EPS = 1e-6
ATT_SCALE = (QK_NOPE + QK_ROPE) ** -0.5
NEG = -1e30

LANES = 128
SUBLANES = 8
HIST_PAD = 32
HEAD_W = N_HEADS * LANES
ROPE_LO = QK_NOPE
HALF = QK_ROPE // 2
R_A, R_GC, R_QC, R_KVC, R_KR, R_GA, R_END = 0, 1024, 1536, 1792, 1920, 1952, 2464

PROJ_ROWS = 512
CONV_ROWS = 64
ATT_TQ = 256
OUT_ROWS = 512
SAMPLE_GROUP = 8
VMEM_LIMIT = 56 * 1024 * 1024

f32 = jnp.float32
bf16 = jnp.bfloat16
NT = (((1,), (1,)), ((), ()))


def _sigmoid(x):
    return 1.0 / (1.0 + jnp.exp(-x))


def _rms(x, g):
    return x * lax.rsqrt(jnp.mean(x * x, axis=-1, keepdims=True) + EPS) * g


def _conv_chunk(gbuf, j, r0, rc, lanes, convw_ref, convb_ref):
    first = HIST_PAD - HIST
    ext = rc + SUBLANES
    y = jnp.broadcast_to(convb_ref[:, lanes], (rc, LANES))
    for s in range(SUBLANES):
        nrows = rc if s == 0 else ext
        part = None
        for o in range(s, first + CONV_WIDTH, SUBLANES):
            if o < first:
                continue
            term = gbuf[j, r0 + o - s:r0 + o - s + nrows, lanes] * convw_ref[o - first:o - first + 1, lanes]
            part = term if part is None else part + term
        y = y + (part if s == 0 else pltpu.roll(part, ext - s, axis=0)[:rc])
    return y


def _proj_body(nb, seg, kr_transposed, x_ref, hist_ref, csq_ref, snq_ref, csk_ref, snk_ref, gpre_ref, win_ref,
               convw_ref, convb_ref, lng_ref, lnb_ref, gqa_ref, wq_ref, gkva_ref, wkv_ref, eone_ref,
               q_ref, k_ref, v_ref, yc_ref, ga_ref, ckv_ref, kr_ref, nh_ref, gbuf, gcbuf):
    rows = nb * seg
    rc = min(CONV_ROWS, seg)
    si = pl.program_id(1)

    x = x_ref[...].reshape(rows, D_MODEL)
    h = _rms(x, gpre_ref[...]).astype(bf16)

    def proj(r0, r1):
        return lax.dot_general(h, win_ref[r0:r1, :], NT, preferred_element_type=f32)

    ab = proj(R_A, R_GC)
    glu = ab[:, :D_CONV] * _sigmoid(ab[:, D_CONV:])

    @pl.when(si == 0)
    def _():
        gbuf[:, 0:HIST_PAD, :] = hist_ref[...]

    gbuf[:, HIST_PAD:HIST_PAD + seg, :] = glu.reshape(nb, seg, D_CONV)

    def conv_rows(j, r0):
        cols = []
        for c in range(D_CONV // LANES):
            cols.append(_conv_chunk(gbuf, j, r0, rc, slice(c * LANES, (c + 1) * LANES), convw_ref, convb_ref))
        y = jnp.concatenate(cols, axis=1)
        mu = jnp.mean(y, axis=-1, keepdims=True)
        d = y - mu
        var = jnp.mean(d * d, axis=-1, keepdims=True)
        yn = d * lax.rsqrt(var + EPS) * lng_ref[...] + lnb_ref[...]
        yc_ref[j, pl.ds(r0, rc), :] = (yn * _sigmoid(yn) * gcbuf[j, pl.ds(r0, rc), :]).astype(bf16)

    t0 = pl.multiple_of(si * seg, seg)
    rows_of = lambda ref: ref[pl.ds(t0, seg), :]
    carry = {}

    def gate_c():
        gc = proj(R_GC, R_QC)
        gcbuf[...] = (gc * _sigmoid(gc)).reshape(nb, seg, D_CONV)

    def gate_a():
        ga = proj(R_GA, R_END)
        ga_ref[...] = (ga * _sigmoid(ga)).astype(bf16).reshape(nb, seg, D_ATT)

    def q_latent():
        carry["qn"] = _rms(proj(R_QC, R_KVC), gqa_ref[...]).astype(bf16)

    def q_heads():
        qq = jnp.dot(carry["qn"], wq_ref[...], preferred_element_type=f32)
        csq, snq = rows_of(csq_ref), rows_of(snq_ref)
        for hh in range(N_HEADS):
            lo = hh * LANES
            qa = qq[:, lo:lo + LANES].reshape(nb, seg, LANES)
            qb = qq[:, HEAD_W + lo:HEAD_W + lo + LANES].reshape(nb, seg, LANES)
            q_ref[:, :, lo:lo + LANES] = (qa * csq + qb * snq).astype(bf16)

    def k_rope():
        kr = proj(R_KR, R_KR + LANES)
        lane = lax.broadcasted_iota(jnp.int32, (rows, LANES), 1)
        kr_rot = jnp.where(lane < HALF, pltpu.roll(kr, LANES - HALF, axis=1), pltpu.roll(kr, HALF, axis=1))
        krope = kr.reshape(nb, seg, LANES) * rows_of(csk_ref) + kr_rot.reshape(nb, seg, LANES) * rows_of(snk_ref)
        if kr_transposed:
            kr_ref[0] = krope[0].T[0:QK_ROPE, :]
        else:
            kr_ref[...] = krope[:, :, 0:QK_ROPE]
        carry["krope_k"] = pltpu.roll(krope.reshape(rows, LANES), ROPE_LO, axis=1)

    def kv_heads():
        ckv = _rms(proj(R_KVC, R_KR), gkva_ref[...])
        ckv_ref[...] = ckv.reshape(nb, seg, KV_LORA)
        kk = jnp.dot(ckv.astype(bf16), wkv_ref[...], preferred_element_type=f32)
        for hh in range(N_HEADS):
            lo = hh * LANES
            k_ref[:, :, lo:lo + LANES] = (kk[:, lo:lo + LANES] + carry["krope_k"]).astype(bf16).reshape(nb, seg, LANES)
            v_ref[:, :, lo:lo + LANES] = (kk[:, HEAD_W + lo:HEAD_W + lo + LANES]
                                          + eone_ref[...]).astype(bf16).reshape(nb, seg, LANES)

    stages = [gate_c, gate_a, q_latent, q_heads, k_rope, kv_heads]
    chunks = [(j, i * rc) for j in range(nb) for i in range(seg // rc)]
    for n, (j, r0) in enumerate(chunks):
        if n < len(stages):
            stages[n]()
        conv_rows(j, r0)
    for stage in stages[len(chunks):]:
        stage()

    tail = gbuf[:, seg:seg + HIST_PAD, :]
    nh_ref[...] = tail
    gbuf[:, 0:HIST_PAD, :] = tail


def _proj_call(x, hist, tables, w, nb, seg, kr_transposed):
    B, T, _ = x.shape
    grid = (B // nb, T // seg)
    full = lambda a: pl.BlockSpec(a.shape, lambda b, s: (0,) * a.ndim)
    tok = lambda width: pl.BlockSpec((nb, seg, width), lambda b, s: (b, s, 0))
    weights = (w["g_pre"], w["w_in_t"], w["conv_w"], w["conv_b"], w["ln_g"], w["ln_b"],
               w["g_qa"], w["w_q"], w["g_kva"], w["w_kv"], w["e_one"])
    if kr_transposed:
        assert nb == 1
        kr_shape, kr_spec = (B, QK_ROPE, T), pl.BlockSpec((1, QK_ROPE, seg), lambda b, s: (b, 0, s))
    else:
        kr_shape, kr_spec = (B, T, QK_ROPE), tok(QK_ROPE)
    out_shape = (
        jax.ShapeDtypeStruct((B, T, HEAD_W), bf16),
        jax.ShapeDtypeStruct((B, T, HEAD_W), bf16),
        jax.ShapeDtypeStruct((B, T, HEAD_W), bf16),
        jax.ShapeDtypeStruct((B, T, D_CONV), bf16),
        jax.ShapeDtypeStruct((B, T, D_ATT), bf16),
        jax.ShapeDtypeStruct((B, T, KV_LORA), f32),
        jax.ShapeDtypeStruct(kr_shape, f32),
        jax.ShapeDtypeStruct((B, HIST_PAD, D_CONV), f32),
    )
    out_specs = (tok(HEAD_W), tok(HEAD_W), tok(HEAD_W), tok(D_CONV), tok(D_ATT), tok(KV_LORA), kr_spec,
                 pl.BlockSpec((nb, HIST_PAD, D_CONV), lambda b, s: (b, 0, 0)))
    return pl.pallas_call(
        functools.partial(_proj_body, nb, seg, kr_transposed),
        grid=grid,
        in_specs=[tok(D_MODEL), pl.BlockSpec((nb, HIST_PAD, D_CONV), lambda b, s: (b, 0, 0))]
        + [full(t) for t in tables] + [full(a) for a in weights],
        out_specs=out_specs,
        out_shape=out_shape,
        scratch_shapes=[pltpu.VMEM((nb, seg + HIST_PAD, D_CONV), f32),
                        pltpu.VMEM((nb, seg, D_CONV), f32)],
        compiler_params=pltpu.CompilerParams(dimension_semantics=("arbitrary", "arbitrary"),
                                             vmem_limit_bytes=VMEM_LIMIT),
        name="proj",
    )(x, hist, *tables, *weights)


def _softmax_pv(parts):
    m = None
    for s, _ in parts:
        mi = jnp.max(s, axis=-1, keepdims=True)
        m = mi if m is None else jnp.maximum(m, mi)
    o = None
    for s, v in parts:
        oi = jnp.dot(jnp.exp(s - m).astype(bf16), v, preferred_element_type=f32)
        o = oi if o is None else o + oi
    return o


def _attend_prompt_body(seq, q_ref, k_ref, v_ref, ga_ref, o_ref):
    tq = ATT_TQ
    row_chunk = lax.broadcasted_iota(jnp.int32, (tq, tq), 0) // CHUNK
    col_chunk = lax.broadcasted_iota(jnp.int32, (tq, tq), 1) // CHUNK
    visible = col_chunk <= row_chunk
    lane = lax.broadcasted_iota(jnp.int32, (tq, LANES), 1)
    def scores(qi, e):
        q0 = qi * tq
        lanes = slice(e * LANES, (e + 1) * LANES)
        qh = q_ref[0, q0:q0 + tq, lanes]
        s_diag = lax.dot_general(qh, k_ref[0, q0:q0 + tq, lanes], NT, preferred_element_type=f32)
        parts = [(jnp.where(visible, s_diag, NEG), v_ref[0, q0:q0 + tq, lanes])]
        if qi > 0:
            s_off = lax.dot_general(qh, k_ref[0, 0:q0, lanes], NT, preferred_element_type=f32)
            parts.append((s_off, v_ref[0, 0:q0, lanes]))
        return parts

    items = [(qi, e) for qi in range(seq // tq) for e in range(2)]
    ahead = 2
    queue = [scores(*it) for it in items[:ahead]]
    heads = []
    for idx, (qi, e) in enumerate(items):
        cur = queue.pop(0)
        if idx + ahead < len(items):
            queue.append(scores(*items[idx + ahead]))
        o = _softmax_pv(cur)
        heads.append(o / o[:, 0:1])
        if e == 1:
            q0 = qi * tq
            y = jnp.where(lane < V_DIM, pltpu.roll(heads[0], V_DIM, axis=1), heads[1])
            o_ref[0, q0:q0 + tq, :] = (y * ga_ref[0, q0:q0 + tq, :].astype(f32)).astype(bf16)
            heads = []


def _attend_prompt_call(q, k, v, ga):
    B, S, _ = q.shape
    pair = pl.BlockSpec((1, S, 2 * LANES), lambda b, p: (b, 0, p))
    half = pl.BlockSpec((1, S, LANES), lambda b, p: (b, 0, p))
    return pl.pallas_call(
        functools.partial(_attend_prompt_body, S),
        grid=(B, N_HEADS // 2),
        in_specs=[pair, pair, pair, half],
        out_specs=half,
        out_shape=jax.ShapeDtypeStruct((B, S, D_ATT), bf16),
        compiler_params=pltpu.CompilerParams(dimension_semantics=("arbitrary", "arbitrary"),
                                             vmem_limit_bytes=VMEM_LIMIT),
        name="attend_prompt",
    )(q, k, v, ga)


def _attend_sample_body(q_ref, ckvn_ref, krn_ref, ckvp_ref, krpt_ref, wabs_ref, wv_ref, ga_ref, o_ref):
    t = q_ref.shape[1]
    q = q_ref[0]
    q_cat = jnp.concatenate(
        [jnp.dot(q[:, hh * LANES:(hh + 1) * LANES], wabs_ref[hh], preferred_element_type=f32)
         for hh in range(N_HEADS)], axis=0).astype(bf16)
    q_lat = q_cat[:, :KV_LORA]
    q_rope = q_cat[:, KV_LORA:KV_LORA + QK_ROPE]
    ckv_p = ckvp_ref[0].astype(bf16)
    ckv_n = ckvn_ref[0].astype(bf16)
    s_p = (lax.dot_general(q_lat, ckv_p, NT, preferred_element_type=f32)
           + jnp.dot(q_rope, krpt_ref[0].astype(bf16), preferred_element_type=f32))
    s_n = (lax.dot_general(q_lat, ckv_n, NT, preferred_element_type=f32)
           + lax.dot_general(q_rope, krn_ref[0].astype(bf16), NT, preferred_element_type=f32))
    m = jnp.maximum(jnp.max(s_p, axis=-1, keepdims=True), jnp.max(s_n, axis=-1, keepdims=True))
    p_p = jnp.exp(s_p - m).astype(bf16)
    p_n = jnp.exp(s_n - m).astype(bf16)
    l = (jnp.sum(p_p.astype(f32), axis=-1, keepdims=True) + jnp.sum(p_n.astype(f32), axis=-1, keepdims=True))
    o_lat = (jnp.dot(p_p, ckv_p, preferred_element_type=f32)
             + jnp.dot(p_n, ckv_n, preferred_element_type=f32)) / l
    o_lat = o_lat.astype(bf16)
    y = None
    for hh in range(N_HEADS):
        yh = jnp.dot(o_lat[hh * t:(hh + 1) * t, :], wv_ref[hh], preferred_element_type=f32)
        y = yh if y is None else y + yh
    o_ref[0] = (y * ga_ref[0].astype(f32)).astype(bf16)


def _attend_sample_call(q, ckv_new, kr_new, ckv_past, kr_past_t, w, ga):
    B, T, _ = q.shape
    P = ckv_past.shape[1]
    per_b = lambda shape: pl.BlockSpec((1,) + shape, lambda b: (b, 0, 0))
    full = lambda a: pl.BlockSpec(a.shape, lambda b: (0,) * a.ndim)
    return pl.pallas_call(
        _attend_sample_body,
        grid=(B,),
        in_specs=[per_b((T, HEAD_W)), per_b((T, KV_LORA)), per_b((T, QK_ROPE)),
                  per_b((P, KV_LORA)), per_b((QK_ROPE, P)), full(w["w_abs"]), full(w["w_vh"]),
                  per_b((T, D_ATT))],
        out_specs=per_b((T, D_ATT)),
        out_shape=jax.ShapeDtypeStruct((B, T, D_ATT), bf16),
        compiler_params=pltpu.CompilerParams(dimension_semantics=("arbitrary",),
                                             vmem_limit_bytes=VMEM_LIMIT),
        name="attend_sample",
    )(q, ckv_new, kr_new, ckv_past, kr_past_t, w["w_abs"], w["w_vh"], ga)


def _out_body(yc_ref, ya_ref, x_ref, wout_ref, gpost_ref, y_ref):
    mixed = (jnp.dot(yc_ref[...], wout_ref[0:D_CONV, :], preferred_element_type=f32)
             + jnp.dot(ya_ref[...], wout_ref[D_CONV:, :], preferred_element_type=f32))
    y_ref[...] = x_ref[...] + _rms(mixed, gpost_ref[...])


def _out_call(yc, ya, x, w):
    n = x.shape[0]
    rows = min(OUT_ROWS, n)
    tok = lambda width: pl.BlockSpec((rows, width), lambda i: (i, 0))
    full = lambda a: pl.BlockSpec(a.shape, lambda i: (0,) * a.ndim)
    return pl.pallas_call(
        _out_body,
        grid=(n // rows,),
        in_specs=[tok(D_CONV), tok(D_ATT), tok(D_MODEL), full(w["w_out"]), full(w["g_post"])],
        out_specs=tok(D_MODEL),
        out_shape=jax.ShapeDtypeStruct((n, D_MODEL), f32),
        compiler_params=pltpu.CompilerParams(dimension_semantics=("arbitrary",),
                                             vmem_limit_bytes=VMEM_LIMIT),
        name="out",
    )(yc, ya, x, w["w_out"], w["g_post"])


def _rot_cols(rope):
    return jnp.concatenate([-rope[..., HALF:], rope[..., :HALF]], axis=-1)


def _layout_weights(g_pre, w_in, conv_w, conv_b, ln_g, ln_b, g_qa, w_qb, g_kva, w_kvb, w_out, g_post):
    dq = QK_NOPE + QK_ROPE
    pad_q = ((0, 0), (0, 0), (0, LANES - dq))
    wq3 = w_qb.reshape(Q_LORA, N_HEADS, dq)
    w_q_plain = jnp.pad(wq3, pad_q).reshape(Q_LORA, HEAD_W)
    w_q_rot = jnp.pad(_rot_cols(wq3[:, :, QK_NOPE:]), ((0, 0), (0, 0), (ROPE_LO, LANES - dq))).reshape(Q_LORA, HEAD_W)
    w_q = jnp.concatenate([w_q_plain, w_q_rot], axis=1)

    lane = jnp.arange(HEAD_W) % LANES
    w_kv = jnp.concatenate([jnp.where(lane < QK_NOPE, w_kvb, 0.0), jnp.where(lane >= QK_NOPE, w_kvb, 0.0)], axis=1)
    e_one = (jnp.arange(LANES) == 0).astype(f32)[None]

    wkv3 = w_kvb.reshape(KV_LORA, N_HEADS, QK_NOPE + V_DIM)
    wk_t = jnp.transpose(wkv3[:, :, :QK_NOPE], (1, 2, 0))
    wk_t = jnp.pad(wk_t, ((0, 0), (0, LANES - QK_NOPE), (0, 0)))
    sel = np.zeros((LANES, LANES), np.float32)
    sel[ROPE_LO + np.arange(QK_ROPE), np.arange(QK_ROPE)] = 1.0
    w_abs = jnp.concatenate([wk_t, jnp.broadcast_to(jnp.asarray(sel), (N_HEADS, LANES, LANES))], axis=2)
    wv = jnp.transpose(wkv3[:, :, QK_NOPE:], (1, 0, 2))
    w_vh = (wv[:, :, None, :] * jnp.eye(N_HEADS, dtype=f32)[:, None, :, None]).reshape(N_HEADS, KV_LORA, D_ATT)

    row = lambda v: v.reshape(1, -1)
    return {
        "g_pre": row(g_pre), "w_in_t": jnp.swapaxes(w_in, 0, 1).astype(bf16), "conv_w": conv_w,
        "conv_b": row(conv_b), "ln_g": row(ln_g), "ln_b": row(ln_b), "g_qa": row(g_qa), "w_q": w_q.astype(bf16),
        "g_kva": row(g_kva), "w_kv": w_kv.astype(bf16), "e_one": e_one,
        "w_abs": w_abs.astype(bf16), "w_vh": w_vh.astype(bf16),
        "w_out": w_out.astype(bf16), "g_post": row(g_post),
    }


def _rope_tables(pos):
    inv = ROPE_THETA ** (-jnp.arange(0, QK_ROPE, 2, dtype=f32) / QK_ROPE)
    ang = pos.astype(f32)[:, None] * inv[None, :]
    cos, sin = jnp.cos(ang), jnp.sin(ang)
    t = pos.shape[0]
    z = lambda n: jnp.zeros((t, n), f32)
    tail = LANES - QK_NOPE - QK_ROPE
    cs_q = jnp.concatenate([jnp.full((t, QK_NOPE), ATT_SCALE, f32), cos * ATT_SCALE, cos * ATT_SCALE, z(tail)], axis=1)
    sn_q = jnp.concatenate([z(QK_NOPE), sin * ATT_SCALE, sin * ATT_SCALE, z(tail)], axis=1)
    cs_k = jnp.concatenate([cos, cos, z(LANES - QK_ROPE)], axis=1)
    sn_k = jnp.concatenate([-sin, sin, z(LANES - QK_ROPE)], axis=1)
    return cs_q, sn_q, cs_k, sn_k


def _layer(x_prompt, x_sample, ckv_past, kr_past, conv_state, w):
    B, S, _ = x_prompt.shape
    Bs, T, _ = x_sample.shape
    past = ckv_past.shape[1]
    pad_hist = lambda hst: jnp.pad(hst, ((0, 0), (HIST_PAD - HIST, 0), (0, 0)))

    qp, kp, vp, ycp, gap, ckvp, krp_t, nhp = _proj_call(
        x_prompt, jnp.zeros((B, HIST_PAD, D_CONV), f32), _rope_tables(jnp.arange(S)), w, 1, PROJ_ROWS, True)
    yap = _attend_prompt_call(qp, kp, vp, gap)
    y_prompt = _out_call(ycp.reshape(B * S, D_CONV), yap.reshape(B * S, D_ATT),
                         x_prompt.reshape(B * S, D_MODEL), w).reshape(B, S, D_MODEL)

    qs, _, _, ycs, gas, ckvs, krs, nhs = _proj_call(
        x_sample, pad_hist(conv_state), _rope_tables(past + jnp.arange(T)), w, SAMPLE_GROUP, T, False)
    yas = _attend_sample_call(qs, ckvs, krs, ckv_past, jnp.swapaxes(kr_past, 1, 2), w, gas)
    y_sample = _out_call(ycs.reshape(Bs * T, D_CONV), yas.reshape(Bs * T, D_ATT),
                         x_sample.reshape(Bs * T, D_MODEL), w).reshape(Bs, T, D_MODEL)

    trim = lambda nh: nh[:, HIST_PAD - HIST:, :]
    return y_prompt, y_sample, ckvp, jnp.swapaxes(krp_t, 1, 2), trim(nhp), ckvs, krs, trim(nhs)


def kernel(x_prompt, x_sample, cache_ckv, cache_krope, state_conv, g_pre, w_in, conv_w, conv_b, conv_ln_g,
           conv_ln_b, g_qa, w_qb, g_kva, w_kvb, w_out, g_post):
    depth = w_in.shape[0]
    yp, ys = x_prompt, x_sample
    outs = [[] for _ in range(6)]
    for l in range(depth):
        w = _layout_weights(g_pre[l], w_in[l], conv_w[l], conv_b[l], conv_ln_g[l], conv_ln_b[l], g_qa[l],
                            w_qb[l], g_kva[l], w_kvb[l], w_out[l], g_post[l])
        yp, ys, *caches = _layer(yp, ys, cache_ckv[l], cache_krope[l], state_conv[l], w)
        for dst, val in zip(outs, caches):
            dst.append(val)
    return (yp, ys) + tuple(jnp.stack(o) for o in outs)
```

```python
import functools

import jax
import jax.numpy as jnp
import numpy as np
from jax import lax
from jax.experimental import pallas as pl
from jax.experimental.pallas import tpu as pltpu

D_MODEL = 1024
CHUNK = 64
D_CONV = 512
D_ATT = 512
CONV_WIDTH = 31
HIST = CONV_WIDTH - 1
N_HEADS = 8
QK_NOPE = 64
QK_ROPE = 32
V_DIM = 64
Q_LORA = 256
KV_LORA = 128
ROPE_THETA = 10000.0
EPS = 1e-6
ATT_SCALE = (QK_NOPE + QK_ROPE) ** -0.5
NEG = -1e30

LANES = 128
SUBLANES = 8
BF16_ROWS = 16
HIST_PAD = 32
HEAD_W = N_HEADS * LANES
ROPE_LO = QK_NOPE
HALF = QK_ROPE // 2
R_A, R_GC, R_QC, R_KVC, R_KR, R_GA, R_END = 0, 1024, 1536, 1792, 1920, 1952, 2464

PROJ_ROWS = 512
CONV_ROWS = 64
ATT_TQ = 256
OUT_ROWS = 1024
SAMPLE_GROUP = 8
VMEM_LIMIT = 56 * 1024 * 1024

f32 = jnp.float32
bf16 = jnp.bfloat16
NT = (((1,), (1,)), ((), ()))


def _sigmoid(x):
    return 0.5 * jnp.tanh(0.5 * x) + 0.5


def _silu(x):
    hx = 0.5 * x
    return hx + hx * jnp.tanh(hx)


def _rms(x, g):
    return x * lax.rsqrt(jnp.mean(x * x, axis=-1, keepdims=True) + EPS) * g


def _conv_chunk(gbuf, j, r0, rc, lanes, convw_ref, convb_ref):
    first = HIST_PAD - HIST
    ext = rc + SUBLANES
    y = jnp.broadcast_to(convb_ref[:, lanes], (rc, LANES))
    for s in range(SUBLANES):
        nrows = rc if s == 0 else ext
        part = None
        for o in range(s, first + CONV_WIDTH, SUBLANES):
            if o < first:
                continue
            term = gbuf[j, pl.ds(r0 + (o - s), nrows), lanes] * convw_ref[o - first:o - first + 1, lanes]
            part = term if part is None else part + term
        y = y + (part if s == 0 else pltpu.roll(part, ext - s, axis=0)[:rc])
    return y


def _proj_body(nb, seg, kr_transposed, zero_ref, x_ref, hist_ref, csq_ref, snq_ref, csk_ref, snk_ref, gpre_ref, win_ref,
               convw_ref, convb_ref, lng_ref, lnb_ref, gqa_ref, wq_ref, gkva_ref, wkv_ref, eone_ref,
               q_ref, k_ref, v_ref, yc_ref, ga_ref, ckv_ref, kr_ref, nh_ref, gbuf, gcbuf, hbuf, ybuf):
    rows = nb * seg
    rc = min(CONV_ROWS, seg)
    si = pl.program_id(1)

    zero = zero_ref[0]
    hrow0 = pl.multiple_of(zero, BF16_ROWS)
    grow0 = pl.multiple_of(zero, SUBLANES)

    def proj(r0, r1):
        return lax.dot_general(hbuf[pl.ds(hrow0, rows), :], win_ref[r0:r1, :], NT, preferred_element_type=f32)

    @pl.when(si == 0)
    def _():
        gbuf[:, 0:HIST_PAD, :] = hist_ref[...]

    half = rows // 2

    def glu_half(part):
        x = (x_ref[0, part * half:(part + 1) * half, :] if nb == 1
             else x_ref[part * (nb // 2):(part + 1) * (nb // 2)].reshape(half, D_MODEL))
        h = _rms(x, gpre_ref[...]).astype(bf16)
        hbuf[part * half:(part + 1) * half, :] = h
        ab = lax.dot_general(h, win_ref[R_A:R_GC, :], NT, preferred_element_type=f32)
        glu = ab[:, :D_CONV] * _sigmoid(ab[:, D_CONV:])
        if nb == 1:
            gbuf[0, HIST_PAD + part * half:HIST_PAD + (part + 1) * half, :] = glu
        else:
            gbuf[part * (nb // 2):(part + 1) * (nb // 2), HIST_PAD:HIST_PAD + seg, :] = glu.reshape(nb // 2, seg, D_CONV)
        return ab

    glu_half(0)

    def conv_rows(j, r0):
        cols = []
        for c in range(D_CONV // LANES):
            cols.append(_conv_chunk(gbuf, j, grow0 + r0, rc, slice(c * LANES, (c + 1) * LANES), convw_ref, convb_ref))
        y = jnp.concatenate(cols, axis=1)
        mu = jnp.mean(y, axis=-1, keepdims=True)
        d = y - mu
        var = jnp.mean(d * d, axis=-1, keepdims=True)
        yn = d * lax.rsqrt(var + EPS) * lng_ref[...] + lnb_ref[...]
        ys = _silu(yn)
        ybuf[j, pl.ds(r0, rc), :] = ys
        return ys[0:BF16_ROWS, 0:LANES].astype(bf16)

    t0 = pl.multiple_of(si * seg, seg)
    rows_of = lambda ref: ref[pl.ds(t0, seg), :]
    carry = {}

    def gate_c():
        gc = proj(R_GC, R_QC)
        gcbuf[...] = _silu(gc).reshape(nb, seg, D_CONV)
        return gc

    def gate_a():
        ga = proj(R_GA, R_END)
        ga_ref[...] = _silu(ga).astype(bf16).reshape(nb, seg, D_ATT)
        return ga

    def q_latent():
        qc = proj(R_QC, R_KVC)
        carry["qn"] = _rms(qc, gqa_ref[...]).astype(bf16)
        return qc

    def q_heads():
        qq = jnp.dot(carry["qn"], wq_ref[...], preferred_element_type=f32)
        csq, snq = rows_of(csq_ref), rows_of(snq_ref)
        for hh in range(N_HEADS):
            lo = hh * LANES
            qa = qq[:, lo:lo + LANES].reshape(nb, seg, LANES)
            qb = qq[:, HEAD_W + lo:HEAD_W + lo + LANES].reshape(nb, seg, LANES)
            q_ref[:, :, lo:lo + LANES] = (qa * csq + qb * snq).astype(bf16)
        return qq

    def k_rope():
        kr = proj(R_KR, R_KR + LANES)
        lane = lax.broadcasted_iota(jnp.int32, (rows, LANES), 1)
        kr_rot = jnp.where(lane < HALF, pltpu.roll(kr, LANES - HALF, axis=1), pltpu.roll(kr, HALF, axis=1))
        krope = kr.reshape(nb, seg, LANES) * rows_of(csk_ref) + kr_rot.reshape(nb, seg, LANES) * rows_of(snk_ref)
        if kr_transposed:
            kr_ref[0] = krope[0].T[0:QK_ROPE, :]
        else:
            kr_ref[...] = krope[:, :, 0:QK_ROPE]
        carry["krope_k"] = pltpu.roll(krope.reshape(rows, LANES), ROPE_LO, axis=1)
        return kr

    def kv_heads():
        ckv = _rms(proj(R_KVC, R_KR), gkva_ref[...])
        ckv_ref[...] = ckv.reshape(nb, seg, KV_LORA)
        kk = jnp.dot(ckv.astype(bf16), wkv_ref[...], preferred_element_type=f32)
        is_k = lax.broadcasted_iota(jnp.int32, (rows, LANES), 1) < QK_NOPE
        for hh in range(N_HEADS):
            kv = kk[:, hh * LANES:(hh + 1) * LANES]
            k_ref[:, :, hh * LANES:(hh + 1) * LANES] = (
                jnp.where(is_k, kv, carry["krope_k"]).astype(bf16).reshape(nb, seg, LANES))
            v_ref[:, :, hh * LANES:(hh + 1) * LANES] = (
                jnp.where(is_k, eone_ref[...], kv).astype(bf16).reshape(nb, seg, LANES))
        return kk

    stages = [functools.partial(glu_half, 1), gate_c, gate_a, q_latent, q_heads, k_rope, kv_heads]
    chunks = [(j, i * rc) for j in range(nb) for i in range(seg // rc)]
    assert len(chunks) >= len(stages) and len(chunks) % 2 == 0
    spare_g = seg + HIST_PAD
    pending = None
    for n, (j, r0) in enumerate(chunks):
        conv_piece = conv_rows(j, r0)
        if pending is not None:
            gbuf[0, spare_g:spare_g + SUBLANES, 0:LANES] = pending[0:SUBLANES, 0:LANES]
            pending = None
        if n < len(stages):
            pending = stages[n]()
        hbuf[rows:rows + BF16_ROWS, 0:LANES] = conv_piece
    yc_ref[...] = (ybuf[...] * gcbuf[...]).astype(bf16)

    tail = gbuf[:, seg:seg + HIST_PAD, :]
    nh_ref[...] = tail
    gbuf[:, 0:HIST_PAD, :] = tail


def _proj_call(x, hist, tables, w, nb, seg, kr_transposed):
    B, T, _ = x.shape
    grid = (B // nb, T // seg)
    full = lambda a: pl.BlockSpec(a.shape, lambda b, s: (0,) * a.ndim)
    tok = lambda width: pl.BlockSpec((nb, seg, width), lambda b, s: (b, s, 0))
    weights = (w["g_pre"], w["w_in_t"], w["conv_w"], w["conv_b"], w["ln_g"], w["ln_b"],
               w["g_qa"], w["w_q"], w["g_kva"], w["w_kv"], w["e_one"])
    if kr_transposed:
        assert nb == 1
        kr_shape, kr_spec = (B, QK_ROPE, T), pl.BlockSpec((1, QK_ROPE, seg), lambda b, s: (b, 0, s))
    else:
        kr_shape, kr_spec = (B, T, QK_ROPE), tok(QK_ROPE)
    out_shape = (
        jax.ShapeDtypeStruct((B, T, HEAD_W), bf16),
        jax.ShapeDtypeStruct((B, T, HEAD_W), bf16),
        jax.ShapeDtypeStruct((B, T, HEAD_W), bf16),
        jax.ShapeDtypeStruct((B, T, D_CONV), bf16),
        jax.ShapeDtypeStruct((B, T, D_ATT), bf16),
        jax.ShapeDtypeStruct((B, T, KV_LORA), f32),
        jax.ShapeDtypeStruct(kr_shape, f32),
        jax.ShapeDtypeStruct((B, HIST_PAD, D_CONV), f32),
    )
    out_specs = (tok(HEAD_W), tok(HEAD_W), tok(HEAD_W), tok(D_CONV), tok(D_ATT), tok(KV_LORA), kr_spec,
                 pl.BlockSpec((nb, HIST_PAD, D_CONV), lambda b, s: (b, 0, 0)))
    return pl.pallas_call(
        functools.partial(_proj_body, nb, seg, kr_transposed),
        grid=grid,
        in_specs=[pl.BlockSpec(memory_space=pltpu.SMEM), tok(D_MODEL),
                  pl.BlockSpec((nb, HIST_PAD, D_CONV), lambda b, s: (b, 0, 0))]
        + [full(t) for t in tables] + [full(a) for a in weights],
        out_specs=out_specs,
        out_shape=out_shape,
        scratch_shapes=[pltpu.VMEM((nb, seg + HIST_PAD + SUBLANES, D_CONV), f32),
                        pltpu.VMEM((nb, seg, D_CONV), f32),
                        pltpu.VMEM((nb * seg + BF16_ROWS, D_MODEL), bf16),
                        pltpu.VMEM((nb, seg, D_CONV), f32)],
        compiler_params=pltpu.CompilerParams(dimension_semantics=("arbitrary", "arbitrary"),
                                             vmem_limit_bytes=VMEM_LIMIT),
        name="proj",
    )(jnp.zeros((1,), jnp.int32), x, hist, *tables, *weights)


def _softmax_pv(parts):
    m = None
    for s, _ in parts:
        mi = jnp.max(s, axis=-1, keepdims=True)
        m = mi if m is None else jnp.maximum(m, mi)
    o = None
    for s, v in parts:
        oi = jnp.dot(jnp.exp(s - m).astype(bf16), v, preferred_element_type=f32)
        o = oi if o is None else o + oi
    return o


def _attend_prompt_body(seq, q_ref, k_ref, v_ref, ga_ref, o_ref):
    tq = ATT_TQ
    row_chunk = lax.broadcasted_iota(jnp.int32, (tq, tq), 0) // CHUNK
    col_chunk = lax.broadcasted_iota(jnp.int32, (tq, tq), 1) // CHUNK
    visible = col_chunk <= row_chunk
    lane = lax.broadcasted_iota(jnp.int32, (tq, LANES), 1)
    def scores(qi, e):
        q0 = qi * tq
        lanes = slice(e * LANES, (e + 1) * LANES)
        qh = q_ref[0, q0:q0 + tq, lanes]
        s_diag = lax.dot_general(qh, k_ref[0, q0:q0 + tq, lanes], NT, preferred_element_type=f32)
        parts = [(jnp.where(visible, s_diag, NEG), v_ref[0, q0:q0 + tq, lanes])]
        if qi > 0:
            s_off = lax.dot_general(qh, k_ref[0, 0:q0, lanes], NT, preferred_element_type=f32)
            parts.append((s_off, v_ref[0, 0:q0, lanes]))
        return parts

    items = [(qi, e) for qi in range(seq // tq) for e in range(2)]
    ahead = 2
    queue = [scores(*it) for it in items[:ahead]]
    heads = []
    for idx, (qi, e) in enumerate(items):
        cur = queue.pop(0)
        if idx + ahead < len(items):
            queue.append(scores(*items[idx + ahead]))
        o = _softmax_pv(cur)
        heads.append(o / o[:, 0:1])
        if e == 1:
            q0 = qi * tq
            y = jnp.where(lane < V_DIM, pltpu.roll(heads[0], V_DIM, axis=1), heads[1])
            o_ref[0, q0:q0 + tq, :] = (y * ga_ref[0, q0:q0 + tq, :].astype(f32)).astype(bf16)
            heads = []


def _attend_prompt_call(q, k, v, ga):
    B, S, _ = q.shape
    pair = pl.BlockSpec((1, S, 2 * LANES), lambda b, p: (b, 0, p))
    half = pl.BlockSpec((1, S, LANES), lambda b, p: (b, 0, p))
    return pl.pallas_call(
        functools.partial(_attend_prompt_body, S),
        grid=(B, N_HEADS // 2),
        in_specs=[pair, pair, pair, half],
        out_specs=half,
        out_shape=jax.ShapeDtypeStruct((B, S, D_ATT), bf16),
        compiler_params=pltpu.CompilerParams(dimension_semantics=("arbitrary", "arbitrary"),
                                             vmem_limit_bytes=VMEM_LIMIT),
        name="attend_prompt",
    )(q, k, v, ga)


def _attend_sample_body(q_ref, ckvn_ref, krn_ref, ckvp_ref, krpt_ref, wabs_ref, wv_ref, ga_ref, o_ref):
    t = q_ref.shape[1]
    q = q_ref[0]
    q_cat = jnp.concatenate(
        [jnp.dot(q[:, hh * LANES:(hh + 1) * LANES], wabs_ref[hh], preferred_element_type=f32)
         for hh in range(N_HEADS)], axis=0).astype(bf16)
    q_lat = q_cat[:, :KV_LORA]
    q_rope = q_cat[:, KV_LORA:KV_LORA + QK_ROPE]
    ckv_p = ckvp_ref[0].astype(bf16)
    ckv_n = ckvn_ref[0].astype(bf16)
    s_p = (lax.dot_general(q_lat, ckv_p, NT, preferred_element_type=f32)
           + jnp.dot(q_rope, krpt_ref[0].astype(bf16), preferred_element_type=f32))
    s_n = (lax.dot_general(q_lat, ckv_n, NT, preferred_element_type=f32)
           + lax.dot_general(q_rope, krn_ref[0].astype(bf16), NT, preferred_element_type=f32))
    m = jnp.maximum(jnp.max(s_p, axis=-1, keepdims=True), jnp.max(s_n, axis=-1, keepdims=True))
    p_p = jnp.exp(s_p - m).astype(bf16)
    p_n = jnp.exp(s_n - m).astype(bf16)
    l = (jnp.sum(p_p.astype(f32), axis=-1, keepdims=True) + jnp.sum(p_n.astype(f32), axis=-1, keepdims=True))
    o_lat = (jnp.dot(p_p, ckv_p, preferred_element_type=f32)
             + jnp.dot(p_n, ckv_n, preferred_element_type=f32)) / l
    o_lat = o_lat.astype(bf16)
    y = None
    for hh in range(N_HEADS):
        yh = jnp.dot(o_lat[hh * t:(hh + 1) * t, :], wv_ref[hh], preferred_element_type=f32)
        y = yh if y is None else y + yh
    o_ref[0] = (y * ga_ref[0].astype(f32)).astype(bf16)


def _attend_sample_call(q, ckv_new, kr_new, ckv_past, kr_past_t, w, ga):
    B, T, _ = q.shape
    P = ckv_past.shape[1]
    per_b = lambda shape: pl.BlockSpec((1,) + shape, lambda b: (b, 0, 0))
    full = lambda a: pl.BlockSpec(a.shape, lambda b: (0,) * a.ndim)
    return pl.pallas_call(
        _attend_sample_body,
        grid=(B,),
        in_specs=[per_b((T, HEAD_W)), per_b((T, KV_LORA)), per_b((T, QK_ROPE)),
                  per_b((P, KV_LORA)), per_b((QK_ROPE, P)), full(w["w_abs"]), full(w["w_vh"]),
                  per_b((T, D_ATT))],
        out_specs=per_b((T, D_ATT)),
        out_shape=jax.ShapeDtypeStruct((B, T, D_ATT), bf16),
        compiler_params=pltpu.CompilerParams(dimension_semantics=("arbitrary",),
                                             vmem_limit_bytes=VMEM_LIMIT),
        name="attend_sample",
    )(q, ckv_new, kr_new, ckv_past, kr_past_t, w["w_abs"], w["w_vh"], ga)


def _out_body(yc_ref, ya_ref, x_ref, wout_ref, gpost_ref, y_ref):
    mixed = (jnp.dot(yc_ref[...], wout_ref[0:D_CONV, :], preferred_element_type=f32)
             + jnp.dot(ya_ref[...], wout_ref[D_CONV:, :], preferred_element_type=f32))
    y_ref[...] = x_ref[...] + _rms(mixed, gpost_ref[...])


def _out_call(yc, ya, x, w):
    n = x.shape[0]
    rows = min(OUT_ROWS, n)
    tok = lambda width: pl.BlockSpec((rows, width), lambda i: (i, 0))
    full = lambda a: pl.BlockSpec(a.shape, lambda i: (0,) * a.ndim)
    return pl.pallas_call(
        _out_body,
        grid=(n // rows,),
        in_specs=[tok(D_CONV), tok(D_ATT), tok(D_MODEL), full(w["w_out"]), full(w["g_post"])],
        out_specs=tok(D_MODEL),
        out_shape=jax.ShapeDtypeStruct((n, D_MODEL), f32),
        compiler_params=pltpu.CompilerParams(dimension_semantics=("arbitrary",),
                                             vmem_limit_bytes=VMEM_LIMIT),
        name="out",
    )(yc, ya, x, w["w_out"], w["g_post"])


def _rot_cols(rope):
    return jnp.concatenate([-rope[..., HALF:], rope[..., :HALF]], axis=-1)


def _layout_weights(g_pre, w_in, conv_w, conv_b, ln_g, ln_b, g_qa, w_qb, g_kva, w_kvb, w_out, g_post):
    dq = QK_NOPE + QK_ROPE
    pad_q = ((0, 0), (0, 0), (0, LANES - dq))
    wq3 = w_qb.reshape(Q_LORA, N_HEADS, dq)
    w_q_plain = jnp.pad(wq3, pad_q).reshape(Q_LORA, HEAD_W)
    w_q_rot = jnp.pad(_rot_cols(wq3[:, :, QK_NOPE:]), ((0, 0), (0, 0), (ROPE_LO, LANES - dq))).reshape(Q_LORA, HEAD_W)
    w_q = jnp.concatenate([w_q_plain, w_q_rot], axis=1)

    w_kv = w_kvb
    e_one = (jnp.arange(LANES) == 0).astype(f32)[None]

    wkv3 = w_kvb.reshape(KV_LORA, N_HEADS, QK_NOPE + V_DIM)
    wk_t = jnp.transpose(wkv3[:, :, :QK_NOPE], (1, 2, 0))
    wk_t = jnp.pad(wk_t, ((0, 0), (0, LANES - QK_NOPE), (0, 0)))
    sel = np.zeros((LANES, LANES), np.float32)
    sel[ROPE_LO + np.arange(QK_ROPE), np.arange(QK_ROPE)] = 1.0
    w_abs = jnp.concatenate([wk_t, jnp.broadcast_to(jnp.asarray(sel), (N_HEADS, LANES, LANES))], axis=2)
    wv = jnp.transpose(wkv3[:, :, QK_NOPE:], (1, 0, 2))
    w_vh = (wv[:, :, None, :] * jnp.eye(N_HEADS, dtype=f32)[:, None, :, None]).reshape(N_HEADS, KV_LORA, D_ATT)

    row = lambda v: v.reshape(1, -1)
    return {
        "g_pre": row(g_pre), "w_in_t": jnp.swapaxes(w_in, 0, 1).astype(bf16), "conv_w": conv_w,
        "conv_b": row(conv_b), "ln_g": row(ln_g), "ln_b": row(ln_b), "g_qa": row(g_qa), "w_q": w_q.astype(bf16),
        "g_kva": row(g_kva), "w_kv": w_kv.astype(bf16), "e_one": e_one,
        "w_abs": w_abs.astype(bf16), "w_vh": w_vh.astype(bf16),
        "w_out": w_out.astype(bf16), "g_post": row(g_post),
    }


def _rope_tables(pos):
    inv = ROPE_THETA ** (-jnp.arange(0, QK_ROPE, 2, dtype=f32) / QK_ROPE)
    ang = pos.astype(f32)[:, None] * inv[None, :]
    cos, sin = jnp.cos(ang), jnp.sin(ang)
    t = pos.shape[0]
    z = lambda n: jnp.zeros((t, n), f32)
    tail = LANES - QK_NOPE - QK_ROPE
    cs_q = jnp.concatenate([jnp.full((t, QK_NOPE), ATT_SCALE, f32), cos * ATT_SCALE, cos * ATT_SCALE, z(tail)], axis=1)
    sn_q = jnp.concatenate([z(QK_NOPE), sin * ATT_SCALE, sin * ATT_SCALE, z(tail)], axis=1)
    cs_k = jnp.concatenate([cos, cos, z(LANES - QK_ROPE)], axis=1)
    sn_k = jnp.concatenate([-sin, sin, z(LANES - QK_ROPE)], axis=1)
    return cs_q, sn_q, cs_k, sn_k


def _layer(x_prompt, x_sample, ckv_past, kr_past, conv_state, w):
    B, S, _ = x_prompt.shape
    Bs, T, _ = x_sample.shape
    past = ckv_past.shape[1]
    pad_hist = lambda hst: jnp.pad(hst, ((0, 0), (HIST_PAD - HIST, 0), (0, 0)))

    qp, kp, vp, ycp, gap, ckvp, krp_t, nhp = _proj_call(
        x_prompt, jnp.zeros((B, HIST_PAD, D_CONV), f32), _rope_tables(jnp.arange(S)), w, 1, PROJ_ROWS, True)
    yap = _attend_prompt_call(qp, kp, vp, gap)
    y_prompt = _out_call(ycp.reshape(B * S, D_CONV), yap.reshape(B * S, D_ATT),
                         x_prompt.reshape(B * S, D_MODEL), w).reshape(B, S, D_MODEL)

    qs, _, _, ycs, gas, ckvs, krs, nhs = _proj_call(
        x_sample, pad_hist(conv_state), _rope_tables(past + jnp.arange(T)), w, SAMPLE_GROUP, T, False)
    yas = _attend_sample_call(qs, ckvs, krs, ckv_past, jnp.swapaxes(kr_past, 1, 2), w, gas)
    y_sample = _out_call(ycs.reshape(Bs * T, D_CONV), yas.reshape(Bs * T, D_ATT),
                         x_sample.reshape(Bs * T, D_MODEL), w).reshape(Bs, T, D_MODEL)

    trim = lambda nh: nh[:, HIST_PAD - HIST:, :]
    return y_prompt, y_sample, ckvp, jnp.swapaxes(krp_t, 1, 2), trim(nhp), ckvs, krs, trim(nhs)


def kernel(x_prompt, x_sample, cache_ckv, cache_krope, state_conv, g_pre, w_in, conv_w, conv_b, conv_ln_g,
           conv_ln_b, g_qa, w_qb, g_kva, w_kvb, w_out, g_post):
    depth = w_in.shape[0]
    yp, ys = x_prompt, x_sample
    outs = [[] for _ in range(6)]
    for l in range(depth):
        w = _layout_weights(g_pre[l], w_in[l], conv_w[l], conv_b[l], conv_ln_g[l], conv_ln_b[l], g_qa[l],
                            w_qb[l], g_kva[l], w_kvb[l], w_out[l], g_post[l])
        yp, ys, *caches = _layer(yp, ys, cache_ckv[l], cache_krope[l], state_conv[l], w)
        for dst, val in zip(outs, caches):
            dst.append(val)
    return (yp, ys) + tuple(jnp.stack(o) for o in outs)
```

```python
import functools

import jax
import jax.numpy as jnp
import numpy as np
from jax import lax
from jax.experimental import pallas as pl
from jax.experimental.pallas import tpu as pltpu

D_MODEL = 1024
CHUNK = 64
D_CONV = 512
D_ATT = 512
CONV_WIDTH = 31
HIST = CONV_WIDTH - 1
N_HEADS = 8
QK_NOPE = 64
QK_ROPE = 32
V_DIM = 64
Q_LORA = 256
KV_LORA = 128
ROPE_THETA = 10000.0
EPS = 1e-6
ATT_SCALE = (QK_NOPE + QK_ROPE) ** -0.5
NEG = -1e30

LANES = 128
SUBLANES = 8
BF16_ROWS = 16
HIST_PAD = 32
HEAD_W = N_HEADS * LANES
ROPE_LO = QK_NOPE
HALF = QK_ROPE // 2
R_A, R_GC, R_QC, R_KVC, R_KR, R_GA, R_END = 0, 1024, 1536, 1792, 1920, 1952, 2464

PROJ_ROWS = 512
CONV_ROWS = 64
ATT_TQ = 256
OUT_ROWS = 1024
SAMPLE_GROUP = 8
SAMPLE_ATT_GROUP = 4
VMEM_LIMIT = 56 * 1024 * 1024

f32 = jnp.float32
bf16 = jnp.bfloat16
NT = (((1,), (1,)), ((), ()))


def _sigmoid(x):
    return 0.5 * jnp.tanh(0.5 * x) + 0.5


def _silu(x):
    hx = 0.5 * x
    return hx + hx * jnp.tanh(hx)


def _rms(x, g):
    return x * lax.rsqrt(jnp.mean(x * x, axis=-1, keepdims=True) + EPS) * g


def _conv_chunk(gbuf, j, r0, rc, lanes, convw_ref, convb_ref):
    first = HIST_PAD - HIST
    ext = rc + SUBLANES
    y = jnp.broadcast_to(convb_ref[:, lanes], (rc, LANES))
    for s in range(SUBLANES):
        nrows = rc if s == 0 else ext
        part = None
        for o in range(s, first + CONV_WIDTH, SUBLANES):
            if o < first:
                continue
            term = gbuf[j, pl.ds(r0 + (o - s), nrows), lanes] * convw_ref[o - first:o - first + 1, lanes]
            part = term if part is None else part + term
        y = y + (part if s == 0 else pltpu.roll(part, ext - s, axis=0)[:rc])
    return y


def _proj_body(nb, seg, kr_transposed, zero_ref, x_ref, hist_ref, csq_ref, snq_ref, csk_ref, snk_ref, gpre_ref, win_ref,
               convw_ref, convb_ref, lng_ref, lnb_ref, gqa_ref, wq_ref, gkva_ref, wkv_ref, eone_ref,
               q_ref, k_ref, v_ref, yc_ref, ga_ref, ckv_ref, kr_ref, nh_ref, gbuf, gcbuf, hbuf, ybuf):
    rows = nb * seg
    rc = min(CONV_ROWS, seg)
    si = pl.program_id(1)

    zero = zero_ref[0]
    hrow0 = pl.multiple_of(zero, BF16_ROWS)
    grow0 = pl.multiple_of(zero, SUBLANES)

    def proj(r0, r1):
        return lax.dot_general(hbuf[pl.ds(hrow0, rows), :], win_ref[r0:r1, :], NT, preferred_element_type=f32)

    @pl.when(si == 0)
    def _():
        gbuf[:, 0:HIST_PAD, :] = hist_ref[...]

    half = rows // 2

    def glu_half(part):
        x = (x_ref[0, part * half:(part + 1) * half, :] if nb == 1
             else x_ref[part * (nb // 2):(part + 1) * (nb // 2)].reshape(half, D_MODEL))
        h = _rms(x, gpre_ref[...]).astype(bf16)
        hbuf[part * half:(part + 1) * half, :] = h
        ab = lax.dot_general(h, win_ref[R_A:R_GC, :], NT, preferred_element_type=f32)
        glu = ab[:, :D_CONV] * _sigmoid(ab[:, D_CONV:])
        if nb == 1:
            gbuf[0, HIST_PAD + part * half:HIST_PAD + (part + 1) * half, :] = glu
        else:
            gbuf[part * (nb // 2):(part + 1) * (nb // 2), HIST_PAD:HIST_PAD + seg, :] = glu.reshape(nb // 2, seg, D_CONV)
        return ab

    glu_half(0)

    def conv_rows(j, r0):
        cols = []
        for c in range(D_CONV // LANES):
            cols.append(_conv_chunk(gbuf, j, grow0 + r0, rc, slice(c * LANES, (c + 1) * LANES), convw_ref, convb_ref))
        y = jnp.concatenate(cols, axis=1)
        mu = jnp.mean(y, axis=-1, keepdims=True)
        d = y - mu
        var = jnp.mean(d * d, axis=-1, keepdims=True)
        yn = d * lax.rsqrt(var + EPS) * lng_ref[...] + lnb_ref[...]
        ys = _silu(yn)
        ybuf[j, pl.ds(r0, rc), :] = ys
        return ys[0:BF16_ROWS, 0:LANES].astype(bf16)

    t0 = pl.multiple_of(si * seg, seg)
    rows_of = lambda ref: ref[pl.ds(t0, seg), :]
    carry = {}

    def gate_c():
        gc = proj(R_GC, R_QC)
        gcbuf[...] = _silu(gc).reshape(nb, seg, D_CONV)
        return gc

    def gate_a():
        ga = proj(R_GA, R_END)
        ga_ref[...] = _silu(ga).astype(bf16).reshape(nb, seg, D_ATT)
        return ga

    def q_latent():
        qc = proj(R_QC, R_KVC)
        carry["qn"] = _rms(qc, gqa_ref[...]).astype(bf16)
        return qc

    def q_heads():
        qq = jnp.dot(carry["qn"], wq_ref[...], preferred_element_type=f32)
        csq, snq = rows_of(csq_ref), rows_of(snq_ref)
        for hh in range(N_HEADS):
            lo = hh * LANES
            qa = qq[:, lo:lo + LANES].reshape(nb, seg, LANES)
            qb = qq[:, HEAD_W + lo:HEAD_W + lo + LANES].reshape(nb, seg, LANES)
            q_ref[:, :, lo:lo + LANES] = (qa * csq + qb * snq).astype(bf16)
        return qq

    def k_rope():
        kr = proj(R_KR, R_KR + LANES)
        lane = lax.broadcasted_iota(jnp.int32, (rows, LANES), 1)
        kr_rot = jnp.where(lane < HALF, pltpu.roll(kr, LANES - HALF, axis=1), pltpu.roll(kr, HALF, axis=1))
        krope = kr.reshape(nb, seg, LANES) * rows_of(csk_ref) + kr_rot.reshape(nb, seg, LANES) * rows_of(snk_ref)
        if kr_transposed:
            kr_ref[0] = krope[0].T[0:QK_ROPE, :]
        else:
            kr_ref[...] = krope[:, :, 0:QK_ROPE]
        carry["krope_k"] = pltpu.roll(krope.reshape(rows, LANES), ROPE_LO, axis=1)
        return kr

    def kv_heads():
        ckv = _rms(proj(R_KVC, R_KR), gkva_ref[...])
        ckv_ref[...] = ckv.reshape(nb, seg, KV_LORA)
        kk = jnp.dot(ckv.astype(bf16), wkv_ref[...], preferred_element_type=f32)
        is_k = lax.broadcasted_iota(jnp.int32, (rows, LANES), 1) < QK_NOPE
        for hh in range(N_HEADS):
            kv = kk[:, hh * LANES:(hh + 1) * LANES]
            k_ref[:, :, hh * LANES:(hh + 1) * LANES] = (
                jnp.where(is_k, kv, carry["krope_k"]).astype(bf16).reshape(nb, seg, LANES))
            v_ref[:, :, hh * LANES:(hh + 1) * LANES] = (
                jnp.where(is_k, eone_ref[...], kv).astype(bf16).reshape(nb, seg, LANES))
        return kk

    stages = [functools.partial(glu_half, 1), gate_c, gate_a, q_latent, q_heads, k_rope, kv_heads]
    chunks = [(j, i * rc) for j in range(nb) for i in range(seg // rc)]
    assert len(chunks) >= len(stages) and len(chunks) % 2 == 0
    spare_g = seg + HIST_PAD
    pending = None
    for n, (j, r0) in enumerate(chunks):
        conv_piece = conv_rows(j, r0)
        if pending is not None:
            gbuf[0, spare_g:spare_g + SUBLANES, 0:LANES] = pending[0:SUBLANES, 0:LANES]
            pending = None
        if n < len(stages):
            pending = stages[n]()
        hbuf[rows:rows + BF16_ROWS, 0:LANES] = conv_piece
    yc_ref[...] = (ybuf[...] * gcbuf[...]).astype(bf16)

    tail = gbuf[:, seg:seg + HIST_PAD, :]
    nh_ref[...] = tail
    gbuf[:, 0:HIST_PAD, :] = tail


def _proj_call(x, hist, tables, w, nb, seg, kr_transposed):
    B, T, _ = x.shape
    grid = (B // nb, T // seg)
    full = lambda a: pl.BlockSpec(a.shape, lambda b, s: (0,) * a.ndim)
    tok = lambda width: pl.BlockSpec((nb, seg, width), lambda b, s: (b, s, 0))
    weights = (w["g_pre"], w["w_in_t"], w["conv_w"], w["conv_b"], w["ln_g"], w["ln_b"],
               w["g_qa"], w["w_q"], w["g_kva"], w["w_kv"], w["e_one"])
    if kr_transposed:
        assert nb == 1
        kr_shape, kr_spec = (B, QK_ROPE, T), pl.BlockSpec((1, QK_ROPE, seg), lambda b, s: (b, 0, s))
    else:
        kr_shape, kr_spec = (B, T, QK_ROPE), tok(QK_ROPE)
    out_shape = (
        jax.ShapeDtypeStruct((B, T, HEAD_W), bf16),
        jax.ShapeDtypeStruct((B, T, HEAD_W), bf16),
        jax.ShapeDtypeStruct((B, T, HEAD_W), bf16),
        jax.ShapeDtypeStruct((B, T, D_CONV), bf16),
        jax.ShapeDtypeStruct((B, T, D_ATT), bf16),
        jax.ShapeDtypeStruct((B, T, KV_LORA), f32),
        jax.ShapeDtypeStruct(kr_shape, f32),
        jax.ShapeDtypeStruct((B, HIST_PAD, D_CONV), f32),
    )
    out_specs = (tok(HEAD_W), tok(HEAD_W), tok(HEAD_W), tok(D_CONV), tok(D_ATT), tok(KV_LORA), kr_spec,
                 pl.BlockSpec((nb, HIST_PAD, D_CONV), lambda b, s: (b, 0, 0)))
    return pl.pallas_call(
        functools.partial(_proj_body, nb, seg, kr_transposed),
        grid=grid,
        in_specs=[pl.BlockSpec(memory_space=pltpu.SMEM), tok(D_MODEL),
                  pl.BlockSpec((nb, HIST_PAD, D_CONV), lambda b, s: (b, 0, 0))]
        + [full(t) for t in tables] + [full(a) for a in weights],
        out_specs=out_specs,
        out_shape=out_shape,
        scratch_shapes=[pltpu.VMEM((nb, seg + HIST_PAD + SUBLANES, D_CONV), f32),
                        pltpu.VMEM((nb, seg, D_CONV), f32),
                        pltpu.VMEM((nb * seg + BF16_ROWS, D_MODEL), bf16),
                        pltpu.VMEM((nb, seg, D_CONV), f32)],
        compiler_params=pltpu.CompilerParams(dimension_semantics=("arbitrary", "arbitrary"),
                                             vmem_limit_bytes=VMEM_LIMIT),
        name="proj",
    )(jnp.zeros((1,), jnp.int32), x, hist, *tables, *weights)


def _softmax_pv(parts):
    m = None
    for s, _ in parts:
        mi = jnp.max(s, axis=-1, keepdims=True)
        m = mi if m is None else jnp.maximum(m, mi)
    o = None
    for s, v in parts:
        oi = jnp.dot(jnp.exp(s - m).astype(bf16), v, preferred_element_type=f32)
        o = oi if o is None else o + oi
    return o


def _attend_prompt_body(seq, q_ref, k_ref, v_ref, ga_ref, o_ref):
    tq = ATT_TQ
    row_chunk = lax.broadcasted_iota(jnp.int32, (tq, tq), 0) // CHUNK
    col_chunk = lax.broadcasted_iota(jnp.int32, (tq, tq), 1) // CHUNK
    visible = col_chunk <= row_chunk
    lane = lax.broadcasted_iota(jnp.int32, (tq, LANES), 1)
    def scores(qi, e):
        q0 = qi * tq
        lanes = slice(e * LANES, (e + 1) * LANES)
        qh = q_ref[0, q0:q0 + tq, lanes]
        s_diag = lax.dot_general(qh, k_ref[0, q0:q0 + tq, lanes], NT, preferred_element_type=f32)
        parts = [(jnp.where(visible, s_diag, NEG), v_ref[0, q0:q0 + tq, lanes])]
        if qi > 0:
            s_off = lax.dot_general(qh, k_ref[0, 0:q0, lanes], NT, preferred_element_type=f32)
            parts.append((s_off, v_ref[0, 0:q0, lanes]))
        return parts

    items = [(qi, e) for qi in range(seq // tq) for e in range(2)]
    ahead = 4
    queue = [scores(*it) for it in items[:ahead]]
    heads = []
    for idx, (qi, e) in enumerate(items):
        cur = queue.pop(0)
        if idx + ahead < len(items):
            queue.append(scores(*items[idx + ahead]))
        o = _softmax_pv(cur)
        heads.append(o / o[:, 0:1])
        if e == 1:
            q0 = qi * tq
            y = jnp.where(lane < V_DIM, pltpu.roll(heads[0], V_DIM, axis=1), heads[1])
            o_ref[0, q0:q0 + tq, :] = (y * ga_ref[0, q0:q0 + tq, :].astype(f32)).astype(bf16)
            heads = []


def _attend_prompt_call(q, k, v, ga):
    B, S, _ = q.shape
    pair = pl.BlockSpec((1, S, 2 * LANES), lambda b, p: (b, 0, p))
    half = pl.BlockSpec((1, S, LANES), lambda b, p: (b, 0, p))
    return pl.pallas_call(
        functools.partial(_attend_prompt_body, S),
        grid=(B, N_HEADS // 2),
        in_specs=[pair, pair, pair, half],
        out_specs=half,
        out_shape=jax.ShapeDtypeStruct((B, S, D_ATT), bf16),
        compiler_params=pltpu.CompilerParams(dimension_semantics=("arbitrary", "arbitrary"),
                                             vmem_limit_bytes=VMEM_LIMIT),
        name="attend_prompt",
    )(q, k, v, ga)


def _attend_sample_body(q_ref, ckvn_ref, krn_ref, ckvp_ref, krpt_ref, wabs_ref, wv_ref, ga_ref, o_ref):
    g_seqs, t, _ = q_ref.shape
    q = q_ref[...].reshape(g_seqs * t, HEAD_W)
    q_cat = [jnp.dot(q[:, hh * LANES:(hh + 1) * LANES], wabs_ref[hh], preferred_element_type=f32).astype(bf16)
             for hh in range(N_HEADS)]

    def scores(g):
        qg = jnp.concatenate([qc[g * t:(g + 1) * t] for qc in q_cat], axis=0)
        q_lat = qg[:, :KV_LORA]
        q_rope = qg[:, KV_LORA:KV_LORA + QK_ROPE]
        ckv_p = ckvp_ref[g].astype(bf16)
        ckv_n = ckvn_ref[g].astype(bf16)
        s_p = (lax.dot_general(q_lat, ckv_p, NT, preferred_element_type=f32)
               + jnp.dot(q_rope, krpt_ref[g].astype(bf16), preferred_element_type=f32))
        s_n = (lax.dot_general(q_lat, ckv_n, NT, preferred_element_type=f32)
               + lax.dot_general(q_rope, krn_ref[g].astype(bf16), NT, preferred_element_type=f32))
        return s_p, s_n, ckv_p, ckv_n

    def latent_out(s_p, s_n, ckv_p, ckv_n):
        m = jnp.maximum(jnp.max(s_p, axis=-1, keepdims=True), jnp.max(s_n, axis=-1, keepdims=True))
        p_p = jnp.exp(s_p - m).astype(bf16)
        p_n = jnp.exp(s_n - m).astype(bf16)
        l = jnp.sum(p_p.astype(f32), axis=-1, keepdims=True) + jnp.sum(p_n.astype(f32), axis=-1, keepdims=True)
        o_lat = (jnp.dot(p_p, ckv_p, preferred_element_type=f32) + jnp.dot(p_n, ckv_n, preferred_element_type=f32)) / l
        return o_lat.astype(bf16)

    nxt = scores(0)
    o_lat = []
    for g in range(g_seqs):
        cur = nxt
        if g + 1 < g_seqs:
            nxt = scores(g + 1)
        o_lat.append(latent_out(*cur))
    y = None
    for hh in range(N_HEADS):
        o_h = jnp.concatenate([o[hh * t:(hh + 1) * t] for o in o_lat], axis=0)
        yh = jnp.dot(o_h, wv_ref[hh], preferred_element_type=f32)
        y = yh if y is None else y + yh
    o_ref[...] = (y.reshape(g_seqs, t, D_ATT) * ga_ref[...].astype(f32)).astype(bf16)


def _attend_sample_call(q, ckv_new, kr_new, ckv_past, kr_past_t, w, ga):
    B, T, _ = q.shape
    P = ckv_past.shape[1]
    per_b = lambda shape: pl.BlockSpec((SAMPLE_ATT_GROUP,) + shape, lambda b: (b, 0, 0))
    full = lambda a: pl.BlockSpec(a.shape, lambda b: (0,) * a.ndim)
    return pl.pallas_call(
        _attend_sample_body,
        grid=(B // SAMPLE_ATT_GROUP,),
        in_specs=[per_b((T, HEAD_W)), per_b((T, KV_LORA)), per_b((T, QK_ROPE)),
                  per_b((P, KV_LORA)), per_b((QK_ROPE, P)), full(w["w_abs"]), full(w["w_vh"]),
                  per_b((T, D_ATT))],
        out_specs=per_b((T, D_ATT)),
        out_shape=jax.ShapeDtypeStruct((B, T, D_ATT), bf16),
        compiler_params=pltpu.CompilerParams(dimension_semantics=("arbitrary",),
                                             vmem_limit_bytes=VMEM_LIMIT),
        name="attend_sample",
    )(q, ckv_new, kr_new, ckv_past, kr_past_t, w["w_abs"], w["w_vh"], ga)


def _out_body(yc_ref, ya_ref, x_ref, wout_ref, gpost_ref, y_ref):
    mixed = (jnp.dot(yc_ref[...], wout_ref[0:D_CONV, :], preferred_element_type=f32)
             + jnp.dot(ya_ref[...], wout_ref[D_CONV:, :], preferred_element_type=f32))
    y_ref[...] = x_ref[...] + _rms(mixed, gpost_ref[...])


def _out_call(yc, ya, x, w):
    n = x.shape[0]
    rows = min(OUT_ROWS, n)
    tok = lambda width: pl.BlockSpec((rows, width), lambda i: (i, 0))
    full = lambda a: pl.BlockSpec(a.shape, lambda i: (0,) * a.ndim)
    return pl.pallas_call(
        _out_body,
        grid=(n // rows,),
        in_specs=[tok(D_CONV), tok(D_ATT), tok(D_MODEL), full(w["w_out"]), full(w["g_post"])],
        out_specs=tok(D_MODEL),
        out_shape=jax.ShapeDtypeStruct((n, D_MODEL), f32),
        compiler_params=pltpu.CompilerParams(dimension_semantics=("arbitrary",),
                                             vmem_limit_bytes=VMEM_LIMIT),
        name="out",
    )(yc, ya, x, w["w_out"], w["g_post"])


def _rot_cols(rope):
    return jnp.concatenate([-rope[..., HALF:], rope[..., :HALF]], axis=-1)


def _layout_weights(g_pre, w_in, conv_w, conv_b, ln_g, ln_b, g_qa, w_qb, g_kva, w_kvb, w_out, g_post):
    dq = QK_NOPE + QK_ROPE
    pad_q = ((0, 0), (0, 0), (0, LANES - dq))
    wq3 = w_qb.reshape(Q_LORA, N_HEADS, dq)
    w_q_plain = jnp.pad(wq3, pad_q).reshape(Q_LORA, HEAD_W)
    w_q_rot = jnp.pad(_rot_cols(wq3[:, :, QK_NOPE:]), ((0, 0), (0, 0), (ROPE_LO, LANES - dq))).reshape(Q_LORA, HEAD_W)
    w_q = jnp.concatenate([w_q_plain, w_q_rot], axis=1)

    w_kv = w_kvb
    e_one = (jnp.arange(LANES) == 0).astype(f32)[None]

    wkv3 = w_kvb.reshape(KV_LORA, N_HEADS, QK_NOPE + V_DIM)
    wk_t = jnp.transpose(wkv3[:, :, :QK_NOPE], (1, 2, 0))
    wk_t = jnp.pad(wk_t, ((0, 0), (0, LANES - QK_NOPE), (0, 0)))
    sel = np.zeros((LANES, LANES), np.float32)
    sel[ROPE_LO + np.arange(QK_ROPE), np.arange(QK_ROPE)] = 1.0
    w_abs = jnp.concatenate([wk_t, jnp.broadcast_to(jnp.asarray(sel), (N_HEADS, LANES, LANES))], axis=2)
    wv = jnp.transpose(wkv3[:, :, QK_NOPE:], (1, 0, 2))
    w_vh = (wv[:, :, None, :] * jnp.eye(N_HEADS, dtype=f32)[:, None, :, None]).reshape(N_HEADS, KV_LORA, D_ATT)

    row = lambda v: v.reshape(1, -1)
    return {
        "g_pre": row(g_pre), "w_in_t": jnp.swapaxes(w_in, 0, 1).astype(bf16), "conv_w": conv_w,
        "conv_b": row(conv_b), "ln_g": row(ln_g), "ln_b": row(ln_b), "g_qa": row(g_qa), "w_q": w_q.astype(bf16),
        "g_kva": row(g_kva), "w_kv": w_kv.astype(bf16), "e_one": e_one,
        "w_abs": w_abs.astype(bf16), "w_vh": w_vh.astype(bf16),
        "w_out": w_out.astype(bf16), "g_post": row(g_post),
    }


def _rope_tables(pos):
    inv = ROPE_THETA ** (-jnp.arange(0, QK_ROPE, 2, dtype=f32) / QK_ROPE)
    ang = pos.astype(f32)[:, None] * inv[None, :]
    cos, sin = jnp.cos(ang), jnp.sin(ang)
    t = pos.shape[0]
    z = lambda n: jnp.zeros((t, n), f32)
    tail = LANES - QK_NOPE - QK_ROPE
    cs_q = jnp.concatenate([jnp.full((t, QK_NOPE), ATT_SCALE, f32), cos * ATT_SCALE, cos * ATT_SCALE, z(tail)], axis=1)
    sn_q = jnp.concatenate([z(QK_NOPE), sin * ATT_SCALE, sin * ATT_SCALE, z(tail)], axis=1)
    cs_k = jnp.concatenate([cos, cos, z(LANES - QK_ROPE)], axis=1)
    sn_k = jnp.concatenate([-sin, sin, z(LANES - QK_ROPE)], axis=1)
    return cs_q, sn_q, cs_k, sn_k


def _layer(x_prompt, x_sample, ckv_past, kr_past, conv_state, w):
    B, S, _ = x_prompt.shape
    Bs, T, _ = x_sample.shape
    past = ckv_past.shape[1]
    pad_hist = lambda hst: jnp.pad(hst, ((0, 0), (HIST_PAD - HIST, 0), (0, 0)))

    qp, kp, vp, ycp, gap, ckvp, krp_t, nhp = _proj_call(
        x_prompt, jnp.zeros((B, HIST_PAD, D_CONV), f32), _rope_tables(jnp.arange(S)), w, 1, PROJ_ROWS, True)
    yap = _attend_prompt_call(qp, kp, vp, gap)
    y_prompt = _out_call(ycp.reshape(B * S, D_CONV), yap.reshape(B * S, D_ATT),
                         x_prompt.reshape(B * S, D_MODEL), w).reshape(B, S, D_MODEL)

    qs, _, _, ycs, gas, ckvs, krs, nhs = _proj_call(
        x_sample, pad_hist(conv_state), _rope_tables(past + jnp.arange(T)), w, SAMPLE_GROUP, T, False)
    yas = _attend_sample_call(qs, ckvs, krs, ckv_past, jnp.swapaxes(kr_past, 1, 2), w, gas)
    y_sample = _out_call(ycs.reshape(Bs * T, D_CONV), yas.reshape(Bs * T, D_ATT),
                         x_sample.reshape(Bs * T, D_MODEL), w).reshape(Bs, T, D_MODEL)

    trim = lambda nh: nh[:, HIST_PAD - HIST:, :]
    return y_prompt, y_sample, ckvp, jnp.swapaxes(krp_t, 1, 2), trim(nhp), ckvs, krs, trim(nhs)


def kernel(x_prompt, x_sample, cache_ckv, cache_krope, state_conv, g_pre, w_in, conv_w, conv_b, conv_ln_g,
           conv_ln_b, g_qa, w_qb, g_kva, w_kvb, w_out, g_post):
    depth = w_in.shape[0]
    yp, ys = x_prompt, x_sample
    outs = [[] for _ in range(6)]
    for l in range(depth):
        w = _layout_weights(g_pre[l], w_in[l], conv_w[l], conv_b[l], conv_ln_g[l], conv_ln_b[l], g_qa[l],
                            w_qb[l], g_kva[l], w_kvb[l], w_out[l], g_post[l])
        yp, ys, *caches = _layer(yp, ys, cache_ckv[l], cache_krope[l], state_conv[l], w)
        for dst, val in zip(outs, caches):
            dst.append(val)
    return (yp, ys) + tuple(jnp.stack(o) for o in outs)
```

```python
import functools

import jax
import jax.numpy as jnp
import numpy as np
from jax import lax
from jax.experimental import pallas as pl
from jax.experimental.pallas import tpu as pltpu

D_MODEL = 1024
CHUNK = 64
D_CONV = 512
D_ATT = 512
CONV_WIDTH = 31
HIST = CONV_WIDTH - 1
N_HEADS = 8
QK_NOPE = 64
QK_ROPE = 32
V_DIM = 64
Q_LORA = 256
KV_LORA = 128
ROPE_THETA = 10000.0
EPS = 1e-6
ATT_SCALE = (QK_NOPE + QK_ROPE) ** -0.5
NEG = -1e30

LANES = 128
SUBLANES = 8
BF16_ROWS = 16
HIST_PAD = 32
HEAD_W = N_HEADS * LANES
ROPE_LO = QK_NOPE
HALF = QK_ROPE // 2
R_A, R_GC, R_QC, R_KVC, R_KR, R_GA, R_END = 0, 1024, 1536, 1792, 1920, 1952, 2464

PROJ_ROWS = 512
ATT_TQ = 256
OUT_ROWS = 1024
SAMPLE_GROUP = 8
SAMPLE_ATT_GROUP = 4
VMEM_LIMIT = 56 * 1024 * 1024

f32 = jnp.float32
bf16 = jnp.bfloat16
NT = (((1,), (1,)), ((), ()))


def _sigmoid(x):
    return 0.5 * jnp.tanh(0.5 * x) + 0.5


def _silu(x):
    hx = 0.5 * x
    return hx + hx * jnp.tanh(hx)


def _rms(x, g):
    return x * lax.rsqrt(jnp.mean(x * x, axis=-1, keepdims=True) + EPS) * g


def _block_rows(n):
    rows = -(-n // SUBLANES)
    while rows % 8 == 0:
        rows += 1
    return rows


def _conv_geometry(nb, seg):
    unit_rows = seg // 2 if nb == 1 else seg
    stride = _block_rows(unit_rows)
    return unit_rows, stride, SUBLANES * stride - unit_rows


def _conv_outputs(slab, row0, stride, i0, n, lanes, convw_ref, convb_ref):
    first = HIST_PAD - HIST
    taps = [jnp.broadcast_to(convw_ref[t:t + 1, lanes], (SUBLANES, LANES)) for t in range(CONV_WIDTH)]
    accs = [jnp.broadcast_to(convb_ref[:, lanes], (SUBLANES, LANES))] * n
    for v in range(i0 + first, i0 + n + first + HIST):
        x = slab[pl.ds(row0 + v, SUBLANES, stride=stride), :]
        for k in range(n):
            t = v - (i0 + k) - first
            if 0 <= t < CONV_WIDTH:
                accs[k] = accs[k] + x * taps[t]
    return accs


def _proj_body(nb, seg, kr_transposed, zero_ref, x_ref, hist_ref, csq_ref, snq_ref, csk_ref, snk_ref, gpre_ref, win_ref,
               convw_ref, convb_ref, lng_ref, lnb_ref, gqa_ref, wq_ref, gkva_ref, wkv_ref, eone_ref,
               q_ref, k_ref, v_ref, yc_ref, ga_ref, ckv_ref, kr_ref, nh_ref, gbuf, gcbuf, hbuf, ybuf):
    rows = nb * seg
    si = pl.program_id(1)

    zero = zero_ref[0]
    hrow0 = pl.multiple_of(zero, BF16_ROWS)
    grow0 = pl.multiple_of(zero, SUBLANES)

    def proj(r0, r1):
        return lax.dot_general(hbuf[pl.ds(hrow0, rows), :], win_ref[r0:r1, :], NT, preferred_element_type=f32)

    n_slabs = D_CONV // LANES
    slab_lanes = [slice(c * LANES, (c + 1) * LANES) for c in range(n_slabs)]
    unit_rows, stride, pad = _conv_geometry(nb, seg)
    units = [(0, 0), (0, unit_rows)] if nb == 1 else [(j, 0) for j in range(nb)]

    @pl.when(si == 0)
    def _():
        for c in range(n_slabs):
            gbuf[:, c, 0:HIST_PAD, :] = hist_ref[:, :, slab_lanes[c]]
            for j, base in units if pad else ():
                end = HIST_PAD + base + unit_rows
                gbuf[j, c, end:end + pad, :] = jnp.zeros((pad, LANES), f32)

    half = rows // 2

    def glu_half(part):
        x = (x_ref[0, part * half:(part + 1) * half, :] if nb == 1
             else x_ref[part * (nb // 2):(part + 1) * (nb // 2)].reshape(half, D_MODEL))
        h = _rms(x, gpre_ref[...]).astype(bf16)
        hbuf[part * half:(part + 1) * half, :] = h
        ab = lax.dot_general(h, win_ref[R_A:R_GC, :], NT, preferred_element_type=f32)
        glu = ab[:, :D_CONV] * _sigmoid(ab[:, D_CONV:])
        for c in range(n_slabs):
            if nb == 1:
                gbuf[0, c, HIST_PAD + part * half:HIST_PAD + (part + 1) * half, :] = glu[:, slab_lanes[c]]
            else:
                gbuf[part * (nb // 2):(part + 1) * (nb // 2), c, HIST_PAD:HIST_PAD + seg, :] = (
                    glu[:, slab_lanes[c]].reshape(nb // 2, seg, LANES))
        return ab

    glu_half(0)

    def conv_rows(j, base, i0, n):
        accs = [_conv_outputs(gbuf.at[j, c], grow0 + base, stride, i0, n, slab_lanes[c], convw_ref, convb_ref)
                for c in range(n_slabs)]
        y = jnp.concatenate([jnp.concatenate([accs[c][k] for c in range(n_slabs)], axis=1) for k in range(n)], axis=0)
        mu = jnp.mean(y, axis=-1, keepdims=True)
        d = y - mu
        var = jnp.mean(d * d, axis=-1, keepdims=True)
        yn = d * lax.rsqrt(var + EPS) * lng_ref[...] + lnb_ref[...]
        ys = _silu(yn)
        for k in range(n):
            for c in range(n_slabs):
                ybuf[j, c, pl.ds(base + i0 + k, SUBLANES, stride=stride), :] = (
                    ys[k * SUBLANES:(k + 1) * SUBLANES, slab_lanes[c]])
        return jnp.concatenate([ys[0:SUBLANES, 0:LANES]] * 2, axis=0).astype(bf16)

    t0 = pl.multiple_of(si * seg, seg)
    rows_of = lambda ref: ref[pl.ds(t0, seg), :]
    carry = {}

    def gate_c():
        gc = proj(R_GC, R_QC)
        gcbuf[...] = _silu(gc).reshape(nb, seg, D_CONV)
        return gc

    def gate_a():
        ga = proj(R_GA, R_END)
        ga_ref[...] = _silu(ga).astype(bf16).reshape(nb, seg, D_ATT)
        return ga

    def q_latent():
        qc = proj(R_QC, R_KVC)
        carry["qn"] = _rms(qc, gqa_ref[...]).astype(bf16)
        return qc

    def q_heads():
        qq = jnp.dot(carry["qn"], wq_ref[...], preferred_element_type=f32)
        csq, snq = rows_of(csq_ref), rows_of(snq_ref)
        for hh in range(N_HEADS):
            lo = hh * LANES
            qa = qq[:, lo:lo + LANES].reshape(nb, seg, LANES)
            qb = qq[:, HEAD_W + lo:HEAD_W + lo + LANES].reshape(nb, seg, LANES)
            q_ref[:, :, lo:lo + LANES] = (qa * csq + qb * snq).astype(bf16)
        return qq

    def k_rope():
        kr = proj(R_KR, R_KR + LANES)
        lane = lax.broadcasted_iota(jnp.int32, (rows, LANES), 1)
        kr_rot = jnp.where(lane < HALF, pltpu.roll(kr, LANES - HALF, axis=1), pltpu.roll(kr, HALF, axis=1))
        krope = kr.reshape(nb, seg, LANES) * rows_of(csk_ref) + kr_rot.reshape(nb, seg, LANES) * rows_of(snk_ref)
        if kr_transposed:
            kr_ref[0] = krope[0].T[0:QK_ROPE, :]
        else:
            kr_ref[...] = krope[:, :, 0:QK_ROPE]
        carry["krope_k"] = pltpu.roll(krope.reshape(rows, LANES), ROPE_LO, axis=1)
        return kr

    def kv_heads():
        ckv = _rms(proj(R_KVC, R_KR), gkva_ref[...])
        ckv_ref[...] = ckv.reshape(nb, seg, KV_LORA)
        kk = jnp.dot(ckv.astype(bf16), wkv_ref[...], preferred_element_type=f32)
        is_k = lax.broadcasted_iota(jnp.int32, (rows, LANES), 1) < QK_NOPE
        for hh in range(N_HEADS):
            kv = kk[:, hh * LANES:(hh + 1) * LANES]
            k_ref[:, :, hh * LANES:(hh + 1) * LANES] = (
                jnp.where(is_k, kv, carry["krope_k"]).astype(bf16).reshape(nb, seg, LANES))
            v_ref[:, :, hh * LANES:(hh + 1) * LANES] = (
                jnp.where(is_k, eone_ref[...], kv).astype(bf16).reshape(nb, seg, LANES))
        return kk

    stages = [functools.partial(glu_half, 1), gate_c, gate_a, q_latent, q_heads, k_rope, kv_heads]
    per_unit = max(1, -(-len(stages) // len(units)))
    splits = [(stride * k) // per_unit for k in range(per_unit + 1)]
    chunks = [(j, base, splits[k], splits[k + 1] - splits[k]) for j, base in units for k in range(per_unit)]
    assert len(chunks) >= len(stages)
    spare_g = HIST_PAD + seg + pad
    pending = None
    for n, chunk in enumerate(chunks):
        conv_piece = conv_rows(*chunk)
        if pending is not None:
            gbuf[0, 0, spare_g:spare_g + SUBLANES, :] = pending[0:SUBLANES, 0:LANES]
            pending = None
        if n < len(stages):
            pending = stages[n]()
        hbuf[rows:rows + BF16_ROWS, 0:LANES] = conv_piece
    ys_all = jnp.concatenate([ybuf[:, c, 0:seg, :] for c in range(n_slabs)], axis=2)
    yc_ref[...] = (ys_all * gcbuf[...]).astype(bf16)

    for c in range(n_slabs):
        tail = gbuf[:, c, seg:seg + HIST_PAD, :]
        nh_ref[:, :, slab_lanes[c]] = tail
        gbuf[:, c, 0:HIST_PAD, :] = tail


def _proj_call(x, hist, tables, w, nb, seg, kr_transposed):
    B, T, _ = x.shape
    grid = (B // nb, T // seg)
    pad = _conv_geometry(nb, seg)[2]
    full = lambda a: pl.BlockSpec(a.shape, lambda b, s: (0,) * a.ndim)
    tok = lambda width: pl.BlockSpec((nb, seg, width), lambda b, s: (b, s, 0))
    weights = (w["g_pre"], w["w_in_t"], w["conv_w"], w["conv_b"], w["ln_g"], w["ln_b"],
               w["g_qa"], w["w_q"], w["g_kva"], w["w_kv"], w["e_one"])
    if kr_transposed:
        assert nb == 1
        kr_shape, kr_spec = (B, QK_ROPE, T), pl.BlockSpec((1, QK_ROPE, seg), lambda b, s: (b, 0, s))
    else:
        kr_shape, kr_spec = (B, T, QK_ROPE), tok(QK_ROPE)
    out_shape = (
        jax.ShapeDtypeStruct((B, T, HEAD_W), bf16),
        jax.ShapeDtypeStruct((B, T, HEAD_W), bf16),
        jax.ShapeDtypeStruct((B, T, HEAD_W), bf16),
        jax.ShapeDtypeStruct((B, T, D_CONV), bf16),
        jax.ShapeDtypeStruct((B, T, D_ATT), bf16),
        jax.ShapeDtypeStruct((B, T, KV_LORA), f32),
        jax.ShapeDtypeStruct(kr_shape, f32),
        jax.ShapeDtypeStruct((B, HIST_PAD, D_CONV), f32),
    )
    out_specs = (tok(HEAD_W), tok(HEAD_W), tok(HEAD_W), tok(D_CONV), tok(D_ATT), tok(KV_LORA), kr_spec,
                 pl.BlockSpec((nb, HIST_PAD, D_CONV), lambda b, s: (b, 0, 0)))
    return pl.pallas_call(
        functools.partial(_proj_body, nb, seg, kr_transposed),
        grid=grid,
        in_specs=[pl.BlockSpec(memory_space=pltpu.SMEM), tok(D_MODEL),
                  pl.BlockSpec((nb, HIST_PAD, D_CONV), lambda b, s: (b, 0, 0))]
        + [full(t) for t in tables] + [full(a) for a in weights],
        out_specs=out_specs,
        out_shape=out_shape,
        scratch_shapes=[pltpu.VMEM((nb, D_CONV // LANES, HIST_PAD + seg + pad + SUBLANES, LANES), f32),
                        pltpu.VMEM((nb, seg, D_CONV), f32),
                        pltpu.VMEM((nb * seg + BF16_ROWS, D_MODEL), bf16),
                        pltpu.VMEM((nb, D_CONV // LANES, seg + pad, LANES), f32)],
        compiler_params=pltpu.CompilerParams(dimension_semantics=("arbitrary", "arbitrary"),
                                             vmem_limit_bytes=VMEM_LIMIT),
        name="proj",
    )(jnp.zeros((1,), jnp.int32), x, hist, *tables, *weights)


def _softmax_pv(parts):
    m = None
    for s, _ in parts:
        mi = jnp.max(s, axis=-1, keepdims=True)
        m = mi if m is None else jnp.maximum(m, mi)
    o = None
    for s, v in parts:
        oi = jnp.dot(jnp.exp(s - m).astype(bf16), v, preferred_element_type=f32)
        o = oi if o is None else o + oi
    return o


def _attend_prompt_body(seq, q_ref, k_ref, v_ref, ga_ref, o_ref):
    tq = ATT_TQ
    row_chunk = lax.broadcasted_iota(jnp.int32, (tq, tq), 0) // CHUNK
    col_chunk = lax.broadcasted_iota(jnp.int32, (tq, tq), 1) // CHUNK
    visible = col_chunk <= row_chunk
    lane = lax.broadcasted_iota(jnp.int32, (tq, LANES), 1)
    def scores(qi, e):
        q0 = qi * tq
        lanes = slice(e * LANES, (e + 1) * LANES)
        qh = q_ref[0, q0:q0 + tq, lanes]
        s_diag = lax.dot_general(qh, k_ref[0, q0:q0 + tq, lanes], NT, preferred_element_type=f32)
        parts = [(jnp.where(visible, s_diag, NEG), v_ref[0, q0:q0 + tq, lanes])]
        if qi > 0:
            s_off = lax.dot_general(qh, k_ref[0, 0:q0, lanes], NT, preferred_element_type=f32)
            parts.append((s_off, v_ref[0, 0:q0, lanes]))
        return parts

    items = [(qi, e) for qi in range(seq // tq) for e in range(2)]
    ahead = 4
    queue = [scores(*it) for it in items[:ahead]]
    heads = []
    for idx, (qi, e) in enumerate(items):
        cur = queue.pop(0)
        if idx + ahead < len(items):
            queue.append(scores(*items[idx + ahead]))
        o = _softmax_pv(cur)
        heads.append(o / o[:, 0:1])
        if e == 1:
            q0 = qi * tq
            y = jnp.where(lane < V_DIM, pltpu.roll(heads[0], V_DIM, axis=1), heads[1])
            o_ref[0, q0:q0 + tq, :] = (y * ga_ref[0, q0:q0 + tq, :].astype(f32)).astype(bf16)
            heads = []


def _attend_prompt_call(q, k, v, ga):
    B, S, _ = q.shape
    pair = pl.BlockSpec((1, S, 2 * LANES), lambda b, p: (b, 0, p))
    half = pl.BlockSpec((1, S, LANES), lambda b, p: (b, 0, p))
    return pl.pallas_call(
        functools.partial(_attend_prompt_body, S),
        grid=(B, N_HEADS // 2),
        in_specs=[pair, pair, pair, half],
        out_specs=half,
        out_shape=jax.ShapeDtypeStruct((B, S, D_ATT), bf16),
        compiler_params=pltpu.CompilerParams(dimension_semantics=("arbitrary", "arbitrary"),
                                             vmem_limit_bytes=VMEM_LIMIT),
        name="attend_prompt",
    )(q, k, v, ga)


def _attend_sample_body(q_ref, ckvn_ref, krn_ref, ckvp_ref, krpt_ref, wabs_ref, wv_ref, ga_ref, o_ref):
    g_seqs, t, _ = q_ref.shape
    q = q_ref[...].reshape(g_seqs * t, HEAD_W)
    q_cat = [jnp.dot(q[:, hh * LANES:(hh + 1) * LANES], wabs_ref[hh], preferred_element_type=f32).astype(bf16)
             for hh in range(N_HEADS)]

    def scores(g):
        qg = jnp.concatenate([qc[g * t:(g + 1) * t] for qc in q_cat], axis=0)
        q_lat = qg[:, :KV_LORA]
        q_rope = qg[:, KV_LORA:KV_LORA + QK_ROPE]
        ckv_p = ckvp_ref[g].astype(bf16)
        ckv_n = ckvn_ref[g].astype(bf16)
        s_p = (lax.dot_general(q_lat, ckv_p, NT, preferred_element_type=f32)
               + jnp.dot(q_rope, krpt_ref[g].astype(bf16), preferred_element_type=f32))
        s_n = (lax.dot_general(q_lat, ckv_n, NT, preferred_element_type=f32)
               + lax.dot_general(q_rope, krn_ref[g].astype(bf16), NT, preferred_element_type=f32))
        return s_p, s_n, ckv_p, ckv_n

    def latent_out(s_p, s_n, ckv_p, ckv_n):
        m = jnp.maximum(jnp.max(s_p, axis=-1, keepdims=True), jnp.max(s_n, axis=-1, keepdims=True))
        p_p = jnp.exp(s_p - m).astype(bf16)
        p_n = jnp.exp(s_n - m).astype(bf16)
        l = jnp.sum(p_p.astype(f32), axis=-1, keepdims=True) + jnp.sum(p_n.astype(f32), axis=-1, keepdims=True)
        o_lat = (jnp.dot(p_p, ckv_p, preferred_element_type=f32) + jnp.dot(p_n, ckv_n, preferred_element_type=f32)) / l
        return o_lat.astype(bf16)

    nxt = scores(0)
    o_lat = []
    for g in range(g_seqs):
        cur = nxt
        if g + 1 < g_seqs:
            nxt = scores(g + 1)
        o_lat.append(latent_out(*cur))
    y = None
    for hh in range(N_HEADS):
        o_h = jnp.concatenate([o[hh * t:(hh + 1) * t] for o in o_lat], axis=0)
        yh = jnp.dot(o_h, wv_ref[hh], preferred_element_type=f32)
        y = yh if y is None else y + yh
    o_ref[...] = (y.reshape(g_seqs, t, D_ATT) * ga_ref[...].astype(f32)).astype(bf16)


def _attend_sample_call(q, ckv_new, kr_new, ckv_past, kr_past_t, w, ga):
    B, T, _ = q.shape
    P = ckv_past.shape[1]
    per_b = lambda shape: pl.BlockSpec((SAMPLE_ATT_GROUP,) + shape, lambda b: (b, 0, 0))
    full = lambda a: pl.BlockSpec(a.shape, lambda b: (0,) * a.ndim)
    return pl.pallas_call(
        _attend_sample_body,
        grid=(B // SAMPLE_ATT_GROUP,),
        in_specs=[per_b((T, HEAD_W)), per_b((T, KV_LORA)), per_b((T, QK_ROPE)),
                  per_b((P, KV_LORA)), per_b((QK_ROPE, P)), full(w["w_abs"]), full(w["w_vh"]),
                  per_b((T, D_ATT))],
        out_specs=per_b((T, D_ATT)),
        out_shape=jax.ShapeDtypeStruct((B, T, D_ATT), bf16),
        compiler_params=pltpu.CompilerParams(dimension_semantics=("arbitrary",),
                                             vmem_limit_bytes=VMEM_LIMIT),
        name="attend_sample",
    )(q, ckv_new, kr_new, ckv_past, kr_past_t, w["w_abs"], w["w_vh"], ga)


def _out_body(yc_ref, ya_ref, x_ref, wout_ref, gpost_ref, y_ref):
    mixed = (jnp.dot(yc_ref[...], wout_ref[0:D_CONV, :], preferred_element_type=f32)
             + jnp.dot(ya_ref[...], wout_ref[D_CONV:, :], preferred_element_type=f32))
    y_ref[...] = x_ref[...] + _rms(mixed, gpost_ref[...])


def _out_call(yc, ya, x, w):
    n = x.shape[0]
    rows = min(OUT_ROWS, n)
    tok = lambda width: pl.BlockSpec((rows, width), lambda i: (i, 0))
    full = lambda a: pl.BlockSpec(a.shape, lambda i: (0,) * a.ndim)
    return pl.pallas_call(
        _out_body,
        grid=(n // rows,),
        in_specs=[tok(D_CONV), tok(D_ATT), tok(D_MODEL), full(w["w_out"]), full(w["g_post"])],
        out_specs=tok(D_MODEL),
        out_shape=jax.ShapeDtypeStruct((n, D_MODEL), f32),
        compiler_params=pltpu.CompilerParams(dimension_semantics=("arbitrary",),
                                             vmem_limit_bytes=VMEM_LIMIT),
        name="out",
    )(yc, ya, x, w["w_out"], w["g_post"])


def _rot_cols(rope):
    return jnp.concatenate([-rope[..., HALF:], rope[..., :HALF]], axis=-1)


def _layout_weights(g_pre, w_in, conv_w, conv_b, ln_g, ln_b, g_qa, w_qb, g_kva, w_kvb, w_out, g_post):
    dq = QK_NOPE + QK_ROPE
    pad_q = ((0, 0), (0, 0), (0, LANES - dq))
    wq3 = w_qb.reshape(Q_LORA, N_HEADS, dq)
    w_q_plain = jnp.pad(wq3, pad_q).reshape(Q_LORA, HEAD_W)
    w_q_rot = jnp.pad(_rot_cols(wq3[:, :, QK_NOPE:]), ((0, 0), (0, 0), (ROPE_LO, LANES - dq))).reshape(Q_LORA, HEAD_W)
    w_q = jnp.concatenate([w_q_plain, w_q_rot], axis=1)

    w_kv = w_kvb
    e_one = (jnp.arange(LANES) == 0).astype(f32)[None]

    wkv3 = w_kvb.reshape(KV_LORA, N_HEADS, QK_NOPE + V_DIM)
    wk_t = jnp.transpose(wkv3[:, :, :QK_NOPE], (1, 2, 0))
    wk_t = jnp.pad(wk_t, ((0, 0), (0, LANES - QK_NOPE), (0, 0)))
    sel = np.zeros((LANES, LANES), np.float32)
    sel[ROPE_LO + np.arange(QK_ROPE), np.arange(QK_ROPE)] = 1.0
    w_abs = jnp.concatenate([wk_t, jnp.broadcast_to(jnp.asarray(sel), (N_HEADS, LANES, LANES))], axis=2)
    wv = jnp.transpose(wkv3[:, :, QK_NOPE:], (1, 0, 2))
    w_vh = (wv[:, :, None, :] * jnp.eye(N_HEADS, dtype=f32)[:, None, :, None]).reshape(N_HEADS, KV_LORA, D_ATT)

    row = lambda v: v.reshape(1, -1)
    return {
        "g_pre": row(g_pre), "w_in_t": jnp.swapaxes(w_in, 0, 1).astype(bf16), "conv_w": conv_w,
        "conv_b": row(conv_b), "ln_g": row(ln_g), "ln_b": row(ln_b), "g_qa": row(g_qa), "w_q": w_q.astype(bf16),
        "g_kva": row(g_kva), "w_kv": w_kv.astype(bf16), "e_one": e_one,
        "w_abs": w_abs.astype(bf16), "w_vh": w_vh.astype(bf16),
        "w_out": w_out.astype(bf16), "g_post": row(g_post),
    }


def _rope_tables(pos):
    inv = ROPE_THETA ** (-jnp.arange(0, QK_ROPE, 2, dtype=f32) / QK_ROPE)
    ang = pos.astype(f32)[:, None] * inv[None, :]
    cos, sin = jnp.cos(ang), jnp.sin(ang)
    t = pos.shape[0]
    z = lambda n: jnp.zeros((t, n), f32)
    tail = LANES - QK_NOPE - QK_ROPE
    cs_q = jnp.concatenate([jnp.full((t, QK_NOPE), ATT_SCALE, f32), cos * ATT_SCALE, cos * ATT_SCALE, z(tail)], axis=1)
    sn_q = jnp.concatenate([z(QK_NOPE), sin * ATT_SCALE, sin * ATT_SCALE, z(tail)], axis=1)
    cs_k = jnp.concatenate([cos, cos, z(LANES - QK_ROPE)], axis=1)
    sn_k = jnp.concatenate([-sin, sin, z(LANES - QK_ROPE)], axis=1)
    return cs_q, sn_q, cs_k, sn_k


def _layer(x_prompt, x_sample, ckv_past, kr_past, conv_state, w):
    B, S, _ = x_prompt.shape
    Bs, T, _ = x_sample.shape
    past = ckv_past.shape[1]
    pad_hist = lambda hst: jnp.pad(hst, ((0, 0), (HIST_PAD - HIST, 0), (0, 0)))

    qp, kp, vp, ycp, gap, ckvp, krp_t, nhp = _proj_call(
        x_prompt, jnp.zeros((B, HIST_PAD, D_CONV), f32), _rope_tables(jnp.arange(S)), w, 1, PROJ_ROWS, True)
    yap = _attend_prompt_call(qp, kp, vp, gap)
    y_prompt = _out_call(ycp.reshape(B * S, D_CONV), yap.reshape(B * S, D_ATT),
                         x_prompt.reshape(B * S, D_MODEL), w).reshape(B, S, D_MODEL)

    qs, _, _, ycs, gas, ckvs, krs, nhs = _proj_call(
        x_sample, pad_hist(conv_state), _rope_tables(past + jnp.arange(T)), w, SAMPLE_GROUP, T, False)
    yas = _attend_sample_call(qs, ckvs, krs, ckv_past, jnp.swapaxes(kr_past, 1, 2), w, gas)
    y_sample = _out_call(ycs.reshape(Bs * T, D_CONV), yas.reshape(Bs * T, D_ATT),
                         x_sample.reshape(Bs * T, D_MODEL), w).reshape(Bs, T, D_MODEL)

    trim = lambda nh: nh[:, HIST_PAD - HIST:, :]
    return y_prompt, y_sample, ckvp, jnp.swapaxes(krp_t, 1, 2), trim(nhp), ckvs, krs, trim(nhs)


def kernel(x_prompt, x_sample, cache_ckv, cache_krope, state_conv, g_pre, w_in, conv_w, conv_b, conv_ln_g,
           conv_ln_b, g_qa, w_qb, g_kva, w_kvb, w_out, g_post):
    depth = w_in.shape[0]
    yp, ys = x_prompt, x_sample
    outs = [[] for _ in range(6)]
    for l in range(depth):
        w = _layout_weights(g_pre[l], w_in[l], conv_w[l], conv_b[l], conv_ln_g[l], conv_ln_b[l], g_qa[l],
                            w_qb[l], g_kva[l], w_kvb[l], w_out[l], g_post[l])
        yp, ys, *caches = _layer(yp, ys, cache_ckv[l], cache_krope[l], state_conv[l], w)
        for dst, val in zip(outs, caches):
            dst.append(val)
    return (yp, ys) + tuple(jnp.stack(o) for o in outs)
```

```python
import functools

import jax
import jax.numpy as jnp
import numpy as np
from jax import lax
from jax.experimental import pallas as pl
from jax.experimental.pallas import tpu as pltpu

D_MODEL = 1024
CHUNK = 64
D_CONV = 512
D_ATT = 512
CONV_WIDTH = 31
HIST = CONV_WIDTH - 1
N_HEADS = 8
QK_NOPE = 64
QK_ROPE = 32
V_DIM = 64
Q_LORA = 256
KV_LORA = 128
ROPE_THETA = 10000.0
EPS = 1e-6
ATT_SCALE = (QK_NOPE + QK_ROPE) ** -0.5
NEG = -1e30

LANES = 128
SUBLANES = 8
BF16_ROWS = 16
HIST_PAD = 32
HEAD_W = N_HEADS * LANES
ROPE_LO = QK_NOPE
HALF = QK_ROPE // 2
R_A, R_GC, R_QC, R_KVC, R_KR, R_GA, R_END = 0, 1024, 1536, 1792, 1920, 1952, 2464

PROJ_ROWS = 512
ATT_TQ = 256
OUT_ROWS = 1024
SAMPLE_GROUP = 8
SAMPLE_ATT_GROUP = 4
VMEM_LIMIT = 56 * 1024 * 1024

f32 = jnp.float32
bf16 = jnp.bfloat16
NT = (((1,), (1,)), ((), ()))


def _sigmoid(x):
    return 0.5 * jnp.tanh(0.5 * x) + 0.5


def _silu(x):
    hx = 0.5 * x
    return hx + hx * jnp.tanh(hx)


def _rms(x, g):
    return x * lax.rsqrt(jnp.mean(x * x, axis=-1, keepdims=True) + EPS) * g


def _block_rows(n):
    rows = -(-n // SUBLANES)
    while rows % 8 == 0:
        rows += 1
    return rows


def _conv_geometry(nb, seg):
    unit_rows = seg // 2 if nb == 1 else seg
    stride = _block_rows(unit_rows)
    return unit_rows, stride, SUBLANES * stride - unit_rows


def _conv_outputs(slab, row0, stride, i0, n, lanes, convw_ref, convb_ref):
    first = HIST_PAD - HIST
    taps = [jnp.broadcast_to(convw_ref[t:t + 1, lanes], (SUBLANES, LANES)) for t in range(CONV_WIDTH)]
    accs = [jnp.broadcast_to(convb_ref[:, lanes], (SUBLANES, LANES))] * n
    for v in range(i0 + first, i0 + n + first + HIST):
        x = slab[pl.ds(row0 + v, SUBLANES, stride=stride), :]
        for k in range(n):
            t = v - (i0 + k) - first
            if 0 <= t < CONV_WIDTH:
                accs[k] = accs[k] + x * taps[t]
    return accs


def _proj_body(nb, seg, kr_transposed, zero_ref, x_ref, hist_ref, csq_ref, snq_ref, csk_ref, snk_ref, gpre_ref, win_ref,
               convw_ref, convb_ref, lng_ref, lnb_ref, gqa_ref, wq_ref, gkva_ref, wkv_ref, eone_ref,
               q_ref, k_ref, v_ref, yc_ref, ga_ref, ckv_ref, kr_ref, nh_ref, gbuf, gcbuf, hbuf, ybuf):
    rows = nb * seg
    si = pl.program_id(1)

    zero = zero_ref[0]
    hrow0 = pl.multiple_of(zero, BF16_ROWS)
    grow0 = pl.multiple_of(zero, SUBLANES)

    def proj(r0, r1):
        return lax.dot_general(hbuf[pl.ds(hrow0, rows), :], win_ref[r0:r1, :], NT, preferred_element_type=f32)

    n_slabs = D_CONV // LANES
    slab_lanes = [slice(c * LANES, (c + 1) * LANES) for c in range(n_slabs)]
    unit_rows, stride, pad = _conv_geometry(nb, seg)
    units = [(0, 0), (0, unit_rows)] if nb == 1 else [(j, 0) for j in range(nb)]
    spare_g = HIST_PAD + seg + pad
    spare_c = spare_g + SUBLANES

    @pl.when(si == 0)
    def _():
        for c in range(n_slabs):
            gbuf[:, c, 0:HIST_PAD, :] = hist_ref[:, :, slab_lanes[c]]
            for j, base in units if pad else ():
                end = HIST_PAD + base + unit_rows
                gbuf[j, c, end:end + pad, :] = jnp.zeros((pad, LANES), f32)

    half = rows // 2

    def glu_half(part):
        x = (x_ref[0, part * half:(part + 1) * half, :] if nb == 1
             else x_ref[part * (nb // 2):(part + 1) * (nb // 2)].reshape(half, D_MODEL))
        h = _rms(x, gpre_ref[...]).astype(bf16)
        hbuf[part * half:(part + 1) * half, :] = h
        ab = lax.dot_general(h, win_ref[R_A:R_GC, :], NT, preferred_element_type=f32)
        glu = ab[:, :D_CONV] * _sigmoid(ab[:, D_CONV:])
        for c in range(n_slabs):
            if nb == 1:
                gbuf[0, c, HIST_PAD + part * half:HIST_PAD + (part + 1) * half, :] = glu[:, slab_lanes[c]]
            else:
                gbuf[part * (nb // 2):(part + 1) * (nb // 2), c, HIST_PAD:HIST_PAD + seg, :] = (
                    glu[:, slab_lanes[c]].reshape(nb // 2, seg, LANES))
        return ab

    glu_half(0)

    def conv_rows(j, base, i0, n):
        accs = []
        for c in range(n_slabs):
            accs.append(_conv_outputs(gbuf.at[j, c], grow0 + base, stride, i0, n, slab_lanes[c], convw_ref, convb_ref))
            if c + 1 < n_slabs:
                gbuf[0, 0, spare_c:spare_c + SUBLANES, :] = accs[-1][-1]
        y = jnp.concatenate([jnp.concatenate([accs[c][k] for c in range(n_slabs)], axis=1) for k in range(n)], axis=0)
        mu = jnp.mean(y, axis=-1, keepdims=True)
        d = y - mu
        var = jnp.mean(d * d, axis=-1, keepdims=True)
        yn = d * lax.rsqrt(var + EPS) * lng_ref[...] + lnb_ref[...]
        ys = _silu(yn)
        for k in range(n):
            for c in range(n_slabs):
                ybuf[j, c, pl.ds(base + i0 + k, SUBLANES, stride=stride), :] = (
                    ys[k * SUBLANES:(k + 1) * SUBLANES, slab_lanes[c]])
        return jnp.concatenate([ys[0:SUBLANES, 0:LANES]] * 2, axis=0).astype(bf16)

    t0 = pl.multiple_of(si * seg, seg)
    rows_of = lambda ref: ref[pl.ds(t0, seg), :]
    carry = {}

    def gate_c():
        gc = proj(R_GC, R_QC)
        gcbuf[...] = _silu(gc).reshape(nb, seg, D_CONV)
        return gc

    def gate_a():
        ga = proj(R_GA, R_END)
        ga_ref[...] = _silu(ga).astype(bf16).reshape(nb, seg, D_ATT)
        return ga

    def q_latent():
        qc = proj(R_QC, R_KVC)
        carry["qn"] = _rms(qc, gqa_ref[...]).astype(bf16)
        return qc

    def q_heads():
        qq = jnp.dot(carry["qn"], wq_ref[...], preferred_element_type=f32)
        csq, snq = rows_of(csq_ref), rows_of(snq_ref)
        for hh in range(N_HEADS):
            lo = hh * LANES
            qa = qq[:, lo:lo + LANES].reshape(nb, seg, LANES)
            qb = qq[:, HEAD_W + lo:HEAD_W + lo + LANES].reshape(nb, seg, LANES)
            q_ref[:, :, lo:lo + LANES] = (qa * csq + qb * snq).astype(bf16)
        return qq

    def k_rope():
        kr = proj(R_KR, R_KR + LANES)
        lane = lax.broadcasted_iota(jnp.int32, (rows, LANES), 1)
        kr_rot = jnp.where(lane < HALF, pltpu.roll(kr, LANES - HALF, axis=1), pltpu.roll(kr, HALF, axis=1))
        krope = kr.reshape(nb, seg, LANES) * rows_of(csk_ref) + kr_rot.reshape(nb, seg, LANES) * rows_of(snk_ref)
        if kr_transposed:
            kr_ref[0] = krope[0].T[0:QK_ROPE, :]
        else:
            kr_ref[...] = krope[:, :, 0:QK_ROPE]
        carry["krope_k"] = pltpu.roll(krope.reshape(rows, LANES), ROPE_LO, axis=1)
        return kr

    def kv_heads():
        ckv = _rms(proj(R_KVC, R_KR), gkva_ref[...])
        ckv_ref[...] = ckv.reshape(nb, seg, KV_LORA)
        kk = jnp.dot(ckv.astype(bf16), wkv_ref[...], preferred_element_type=f32)
        is_k = lax.broadcasted_iota(jnp.int32, (rows, LANES), 1) < QK_NOPE
        for hh in range(N_HEADS):
            kv = kk[:, hh * LANES:(hh + 1) * LANES]
            k_ref[:, :, hh * LANES:(hh + 1) * LANES] = (
                jnp.where(is_k, kv, carry["krope_k"]).astype(bf16).reshape(nb, seg, LANES))
            vv = jnp.where(is_k, eone_ref[...], kv)
            if kr_transposed:
                v_ref[0, hh * LANES:(hh + 1) * LANES, :] = vv.T.astype(bf16)
            else:
                v_ref[:, :, hh * LANES:(hh + 1) * LANES] = vv.astype(bf16).reshape(nb, seg, LANES)
        return kk

    stages = [functools.partial(glu_half, 1), gate_c, gate_a, q_latent, q_heads, k_rope, kv_heads]
    per_unit = max(1, -(-len(stages) // len(units)))
    splits = [(stride * k) // per_unit for k in range(per_unit + 1)]
    chunks = [(j, base, splits[k], splits[k + 1] - splits[k]) for j, base in units for k in range(per_unit)]
    assert len(chunks) >= len(stages)
    pending = None
    for n, chunk in enumerate(chunks):
        conv_piece = conv_rows(*chunk)
        if pending is not None:
            gbuf[0, 0, spare_g:spare_g + SUBLANES, :] = pending[0:SUBLANES, 0:LANES]
            pending = None
        if n < len(stages):
            pending = stages[n]()
        hbuf[rows:rows + BF16_ROWS, 0:LANES] = conv_piece
    ys_all = jnp.concatenate([ybuf[:, c, 0:seg, :] for c in range(n_slabs)], axis=2)
    yc_ref[...] = (ys_all * gcbuf[...]).astype(bf16)

    for c in range(n_slabs):
        tail = gbuf[:, c, seg:seg + HIST_PAD, :]
        nh_ref[:, :, slab_lanes[c]] = tail
        gbuf[:, c, 0:HIST_PAD, :] = tail


def _proj_call(x, hist, tables, w, nb, seg, kr_transposed):
    B, T, _ = x.shape
    grid = (B // nb, T // seg)
    pad = _conv_geometry(nb, seg)[2]
    full = lambda a: pl.BlockSpec(a.shape, lambda b, s: (0,) * a.ndim)
    tok = lambda width: pl.BlockSpec((nb, seg, width), lambda b, s: (b, s, 0))
    weights = (w["g_pre"], w["w_in_t"], w["conv_w"], w["conv_b"], w["ln_g"], w["ln_b"],
               w["g_qa"], w["w_q"], w["g_kva"], w["w_kv"], w["e_one"])
    if kr_transposed:
        assert nb == 1
        kr_shape, kr_spec = (B, QK_ROPE, T), pl.BlockSpec((1, QK_ROPE, seg), lambda b, s: (b, 0, s))
        v_shape, v_spec = (B, HEAD_W, T), pl.BlockSpec((1, HEAD_W, seg), lambda b, s: (b, 0, s))
    else:
        kr_shape, kr_spec = (B, T, QK_ROPE), tok(QK_ROPE)
        v_shape, v_spec = (B, T, HEAD_W), tok(HEAD_W)
    out_shape = (
        jax.ShapeDtypeStruct((B, T, HEAD_W), bf16),
        jax.ShapeDtypeStruct((B, T, HEAD_W), bf16),
        jax.ShapeDtypeStruct(v_shape, bf16),
        jax.ShapeDtypeStruct((B, T, D_CONV), bf16),
        jax.ShapeDtypeStruct((B, T, D_ATT), bf16),
        jax.ShapeDtypeStruct((B, T, KV_LORA), f32),
        jax.ShapeDtypeStruct(kr_shape, f32),
        jax.ShapeDtypeStruct((B, HIST_PAD, D_CONV), f32),
    )
    out_specs = (tok(HEAD_W), tok(HEAD_W), v_spec, tok(D_CONV), tok(D_ATT), tok(KV_LORA), kr_spec,
                 pl.BlockSpec((nb, HIST_PAD, D_CONV), lambda b, s: (b, 0, 0)))
    return pl.pallas_call(
        functools.partial(_proj_body, nb, seg, kr_transposed),
        grid=grid,
        in_specs=[pl.BlockSpec(memory_space=pltpu.SMEM), tok(D_MODEL),
                  pl.BlockSpec((nb, HIST_PAD, D_CONV), lambda b, s: (b, 0, 0))]
        + [full(t) for t in tables] + [full(a) for a in weights],
        out_specs=out_specs,
        out_shape=out_shape,
        scratch_shapes=[pltpu.VMEM((nb, D_CONV // LANES, HIST_PAD + seg + pad + 2 * SUBLANES, LANES), f32),
                        pltpu.VMEM((nb, seg, D_CONV), f32),
                        pltpu.VMEM((nb * seg + BF16_ROWS, D_MODEL), bf16),
                        pltpu.VMEM((nb, D_CONV // LANES, seg + pad, LANES), f32)],
        compiler_params=pltpu.CompilerParams(dimension_semantics=("arbitrary", "arbitrary"),
                                             vmem_limit_bytes=VMEM_LIMIT),
        name="proj",
    )(jnp.zeros((1,), jnp.int32), x, hist, *tables, *weights)


def _softmax_pv_t(parts):
    m = None
    for s, _ in parts:
        mi = jnp.max(s, axis=0, keepdims=True)
        m = mi if m is None else jnp.maximum(m, mi)
    o = None
    for s, vt in parts:
        oi = jnp.dot(vt, jnp.exp(s - m).astype(bf16), preferred_element_type=f32)
        o = oi if o is None else o + oi
    return o


def _attend_prompt_body(seq, q_ref, k_ref, vt_ref, ga_ref, o_ref):
    tq = ATT_TQ
    key_chunk = lax.broadcasted_iota(jnp.int32, (tq, tq), 0) // CHUNK
    qry_chunk = lax.broadcasted_iota(jnp.int32, (tq, tq), 1) // CHUNK
    visible = key_chunk <= qry_chunk
    lane = lax.broadcasted_iota(jnp.int32, (tq, LANES), 1)

    def scores(qi, e):
        q0 = qi * tq
        lanes = slice(e * LANES, (e + 1) * LANES)
        qh = q_ref[0, q0:q0 + tq, lanes]
        s_diag = lax.dot_general(k_ref[0, q0:q0 + tq, lanes], qh, NT, preferred_element_type=f32)
        parts = [(jnp.where(visible, s_diag, NEG), vt_ref[0, lanes, q0:q0 + tq])]
        if qi > 0:
            s_off = lax.dot_general(k_ref[0, 0:q0, lanes], qh, NT, preferred_element_type=f32)
            parts.append((s_off, vt_ref[0, lanes, 0:q0]))
        return parts

    items = [(qi, e) for qi in range(seq // tq) for e in range(2)]
    ahead = 4
    queue = [scores(*it) for it in items[:ahead]]
    heads = []
    for idx, (qi, e) in enumerate(items):
        cur = queue.pop(0)
        if idx + ahead < len(items):
            queue.append(scores(*items[idx + ahead]))
        ot = _softmax_pv_t(cur)
        heads.append((ot / ot[0:1, :]).T)
        if e == 1:
            q0 = qi * tq
            y = jnp.where(lane < V_DIM, pltpu.roll(heads[0], V_DIM, axis=1), heads[1])
            o_ref[0, q0:q0 + tq, :] = (y * ga_ref[0, q0:q0 + tq, :].astype(f32)).astype(bf16)
            heads = []


def _attend_prompt_call(q, k, vt, ga):
    B, S, _ = q.shape
    pair = pl.BlockSpec((1, S, 2 * LANES), lambda b, p: (b, 0, p))
    pair_t = pl.BlockSpec((1, 2 * LANES, S), lambda b, p: (b, p, 0))
    half = pl.BlockSpec((1, S, LANES), lambda b, p: (b, 0, p))
    return pl.pallas_call(
        functools.partial(_attend_prompt_body, S),
        grid=(B, N_HEADS // 2),
        in_specs=[pair, pair, pair_t, half],
        out_specs=half,
        out_shape=jax.ShapeDtypeStruct((B, S, D_ATT), bf16),
        compiler_params=pltpu.CompilerParams(dimension_semantics=("arbitrary", "arbitrary"),
                                             vmem_limit_bytes=VMEM_LIMIT),
        name="attend_prompt",
    )(q, k, vt, ga)


def _attend_sample_body(q_ref, ckvn_ref, krn_ref, ckvp_ref, krpt_ref, wabs_ref, wv_ref, ga_ref, o_ref):
    g_seqs, t, _ = q_ref.shape
    q = q_ref[...].reshape(g_seqs * t, HEAD_W)
    q_cat = [jnp.dot(q[:, hh * LANES:(hh + 1) * LANES], wabs_ref[hh], preferred_element_type=f32).astype(bf16)
             for hh in range(N_HEADS)]

    def scores(g):
        qg = jnp.concatenate([qc[g * t:(g + 1) * t] for qc in q_cat], axis=0)
        q_lat = qg[:, :KV_LORA]
        q_rope = qg[:, KV_LORA:KV_LORA + QK_ROPE]
        ckv_p = ckvp_ref[g].astype(bf16)
        ckv_n = ckvn_ref[g].astype(bf16)
        s_p = (lax.dot_general(q_lat, ckv_p, NT, preferred_element_type=f32)
               + jnp.dot(q_rope, krpt_ref[g].astype(bf16), preferred_element_type=f32))
        s_n = (lax.dot_general(q_lat, ckv_n, NT, preferred_element_type=f32)
               + lax.dot_general(q_rope, krn_ref[g].astype(bf16), NT, preferred_element_type=f32))
        return s_p, s_n, ckv_p, ckv_n

    def latent_out(s_p, s_n, ckv_p, ckv_n):
        m = jnp.maximum(jnp.max(s_p, axis=-1, keepdims=True), jnp.max(s_n, axis=-1, keepdims=True))
        p_p = jnp.exp(s_p - m).astype(bf16)
        p_n = jnp.exp(s_n - m).astype(bf16)
        l = jnp.sum(p_p.astype(f32), axis=-1, keepdims=True) + jnp.sum(p_n.astype(f32), axis=-1, keepdims=True)
        o_lat = (jnp.dot(p_p, ckv_p, preferred_element_type=f32) + jnp.dot(p_n, ckv_n, preferred_element_type=f32)) / l
        return o_lat.astype(bf16)

    nxt = scores(0)
    o_lat = []
    for g in range(g_seqs):
        cur = nxt
        if g + 1 < g_seqs:
            nxt = scores(g + 1)
        o_lat.append(latent_out(*cur))
    y = None
    for hh in range(N_HEADS):
        o_h = jnp.concatenate([o[hh * t:(hh + 1) * t] for o in o_lat], axis=0)
        yh = jnp.dot(o_h, wv_ref[hh], preferred_element_type=f32)
        y = yh if y is None else y + yh
    o_ref[...] = (y.reshape(g_seqs, t, D_ATT) * ga_ref[...].astype(f32)).astype(bf16)


def _attend_sample_call(q, ckv_new, kr_new, ckv_past, kr_past_t, w, ga):
    B, T, _ = q.shape
    P = ckv_past.shape[1]
    per_b = lambda shape: pl.BlockSpec((SAMPLE_ATT_GROUP,) + shape, lambda b: (b, 0, 0))
    full = lambda a: pl.BlockSpec(a.shape, lambda b: (0,) * a.ndim)
    return pl.pallas_call(
        _attend_sample_body,
        grid=(B // SAMPLE_ATT_GROUP,),
        in_specs=[per_b((T, HEAD_W)), per_b((T, KV_LORA)), per_b((T, QK_ROPE)),
                  per_b((P, KV_LORA)), per_b((QK_ROPE, P)), full(w["w_abs"]), full(w["w_vh"]),
                  per_b((T, D_ATT))],
        out_specs=per_b((T, D_ATT)),
        out_shape=jax.ShapeDtypeStruct((B, T, D_ATT), bf16),
        compiler_params=pltpu.CompilerParams(dimension_semantics=("arbitrary",),
                                             vmem_limit_bytes=VMEM_LIMIT),
        name="attend_sample",
    )(q, ckv_new, kr_new, ckv_past, kr_past_t, w["w_abs"], w["w_vh"], ga)


def _out_body(yc_ref, ya_ref, x_ref, wout_ref, gpost_ref, y_ref):
    mixed = (jnp.dot(yc_ref[...], wout_ref[0:D_CONV, :], preferred_element_type=f32)
             + jnp.dot(ya_ref[...], wout_ref[D_CONV:, :], preferred_element_type=f32))
    y_ref[...] = x_ref[...] + _rms(mixed, gpost_ref[...])


def _out_call(yc, ya, x, w):
    n = x.shape[0]
    rows = min(OUT_ROWS, n)
    tok = lambda width: pl.BlockSpec((rows, width), lambda i: (i, 0))
    full = lambda a: pl.BlockSpec(a.shape, lambda i: (0,) * a.ndim)
    return pl.pallas_call(
        _out_body,
        grid=(n // rows,),
        in_specs=[tok(D_CONV), tok(D_ATT), tok(D_MODEL), full(w["w_out"]), full(w["g_post"])],
        out_specs=tok(D_MODEL),
        out_shape=jax.ShapeDtypeStruct((n, D_MODEL), f32),
        compiler_params=pltpu.CompilerParams(dimension_semantics=("arbitrary",),
                                             vmem_limit_bytes=VMEM_LIMIT),
        name="out",
    )(yc, ya, x, w["w_out"], w["g_post"])


def _rot_cols(rope):
    return jnp.concatenate([-rope[..., HALF:], rope[..., :HALF]], axis=-1)


def _layout_weights(g_pre, w_in, conv_w, conv_b, ln_g, ln_b, g_qa, w_qb, g_kva, w_kvb, w_out, g_post):
    dq = QK_NOPE + QK_ROPE
    pad_q = ((0, 0), (0, 0), (0, LANES - dq))
    wq3 = w_qb.reshape(Q_LORA, N_HEADS, dq)
    w_q_plain = jnp.pad(wq3, pad_q).reshape(Q_LORA, HEAD_W)
    w_q_rot = jnp.pad(_rot_cols(wq3[:, :, QK_NOPE:]), ((0, 0), (0, 0), (ROPE_LO, LANES - dq))).reshape(Q_LORA, HEAD_W)
    w_q = jnp.concatenate([w_q_plain, w_q_rot], axis=1)

    w_kv = w_kvb
    e_one = (jnp.arange(LANES) == 0).astype(f32)[None]

    wkv3 = w_kvb.reshape(KV_LORA, N_HEADS, QK_NOPE + V_DIM)
    wk_t = jnp.transpose(wkv3[:, :, :QK_NOPE], (1, 2, 0))
    wk_t = jnp.pad(wk_t, ((0, 0), (0, LANES - QK_NOPE), (0, 0)))
    sel = np.zeros((LANES, LANES), np.float32)
    sel[ROPE_LO + np.arange(QK_ROPE), np.arange(QK_ROPE)] = 1.0
    w_abs = jnp.concatenate([wk_t, jnp.broadcast_to(jnp.asarray(sel), (N_HEADS, LANES, LANES))], axis=2)
    wv = jnp.transpose(wkv3[:, :, QK_NOPE:], (1, 0, 2))
    w_vh = (wv[:, :, None, :] * jnp.eye(N_HEADS, dtype=f32)[:, None, :, None]).reshape(N_HEADS, KV_LORA, D_ATT)

    row = lambda v: v.reshape(1, -1)
    return {
        "g_pre": row(g_pre), "w_in_t": jnp.swapaxes(w_in, 0, 1).astype(bf16), "conv_w": conv_w,
        "conv_b": row(conv_b), "ln_g": row(ln_g), "ln_b": row(ln_b), "g_qa": row(g_qa), "w_q": w_q.astype(bf16),
        "g_kva": row(g_kva), "w_kv": w_kv.astype(bf16), "e_one": e_one,
        "w_abs": w_abs.astype(bf16), "w_vh": w_vh.astype(bf16),
        "w_out": w_out.astype(bf16), "g_post": row(g_post),
    }


def _rope_tables(pos):
    inv = ROPE_THETA ** (-jnp.arange(0, QK_ROPE, 2, dtype=f32) / QK_ROPE)
    ang = pos.astype(f32)[:, None] * inv[None, :]
    cos, sin = jnp.cos(ang), jnp.sin(ang)
    t = pos.shape[0]
    z = lambda n: jnp.zeros((t, n), f32)
    tail = LANES - QK_NOPE - QK_ROPE
    cs_q = jnp.concatenate([jnp.full((t, QK_NOPE), ATT_SCALE, f32), cos * ATT_SCALE, cos * ATT_SCALE, z(tail)], axis=1)
    sn_q = jnp.concatenate([z(QK_NOPE), sin * ATT_SCALE, sin * ATT_SCALE, z(tail)], axis=1)
    cs_k = jnp.concatenate([cos, cos, z(LANES - QK_ROPE)], axis=1)
    sn_k = jnp.concatenate([-sin, sin, z(LANES - QK_ROPE)], axis=1)
    return cs_q, sn_q, cs_k, sn_k


def _layer(x_prompt, x_sample, ckv_past, kr_past, conv_state, w):
    B, S, _ = x_prompt.shape
    Bs, T, _ = x_sample.shape
    past = ckv_past.shape[1]
    pad_hist = lambda hst: jnp.pad(hst, ((0, 0), (HIST_PAD - HIST, 0), (0, 0)))

    qp, kp, vp, ycp, gap, ckvp, krp_t, nhp = _proj_call(
        x_prompt, jnp.zeros((B, HIST_PAD, D_CONV), f32), _rope_tables(jnp.arange(S)), w, 1, PROJ_ROWS, True)
    yap = _attend_prompt_call(qp, kp, vp, gap)
    y_prompt = _out_call(ycp.reshape(B * S, D_CONV), yap.reshape(B * S, D_ATT),
                         x_prompt.reshape(B * S, D_MODEL), w).reshape(B, S, D_MODEL)

    qs, _, _, ycs, gas, ckvs, krs, nhs = _proj_call(
        x_sample, pad_hist(conv_state), _rope_tables(past + jnp.arange(T)), w, SAMPLE_GROUP, T, False)
    yas = _attend_sample_call(qs, ckvs, krs, ckv_past, jnp.swapaxes(kr_past, 1, 2), w, gas)
    y_sample = _out_call(ycs.reshape(Bs * T, D_CONV), yas.reshape(Bs * T, D_ATT),
                         x_sample.reshape(Bs * T, D_MODEL), w).reshape(Bs, T, D_MODEL)

    trim = lambda nh: nh[:, HIST_PAD - HIST:, :]
    return y_prompt, y_sample, ckvp, jnp.swapaxes(krp_t, 1, 2), trim(nhp), ckvs, krs, trim(nhs)


def kernel(x_prompt, x_sample, cache_ckv, cache_krope, state_conv, g_pre, w_in, conv_w, conv_b, conv_ln_g,
           conv_ln_b, g_qa, w_qb, g_kva, w_kvb, w_out, g_post):
    depth = w_in.shape[0]
    yp, ys = x_prompt, x_sample
    outs = [[] for _ in range(6)]
    for l in range(depth):
        w = _layout_weights(g_pre[l], w_in[l], conv_w[l], conv_b[l], conv_ln_g[l], conv_ln_b[l], g_qa[l],
                            w_qb[l], g_kva[l], w_kvb[l], w_out[l], g_post[l])
        yp, ys, *caches = _layer(yp, ys, cache_ckv[l], cache_krope[l], state_conv[l], w)
        for dst, val in zip(outs, caches):
            dst.append(val)
    return (yp, ys) + tuple(jnp.stack(o) for o in outs)
```

```python
import functools

import jax
import jax.numpy as jnp
import numpy as np
from jax import lax
from jax.experimental import pallas as pl
from jax.experimental.pallas import tpu as pltpu

D_MODEL = 1024
CHUNK = 64
D_CONV = 512
D_ATT = 512
CONV_WIDTH = 31
HIST = CONV_WIDTH - 1
N_HEADS = 8
QK_NOPE = 64
QK_ROPE = 32
V_DIM = 64
Q_LORA = 256
KV_LORA = 128
ROPE_THETA = 10000.0
EPS = 1e-6
ATT_SCALE = (QK_NOPE + QK_ROPE) ** -0.5
NEG = -1e30

LANES = 128
SUBLANES = 8
BF16_ROWS = 16
HIST_PAD = 32
HEAD_W = N_HEADS * LANES
ROPE_LO = QK_NOPE
HALF = QK_ROPE // 2
R_A, R_GC, R_QC, R_KVC, R_KR, R_GA, R_END = 0, 1024, 1536, 1792, 1920, 1952, 2464

PROJ_ROWS = 512
ATT_TQ = 256
ATT_HEADS = 2
OUT_ROWS = 1024
OUT_SUB_ROWS = 256
SAMPLE_GROUP = 8
SAMPLE_ATT_GROUP = 4
VMEM_LIMIT = 56 * 1024 * 1024

f32 = jnp.float32
bf16 = jnp.bfloat16
NT = (((1,), (1,)), ((), ()))


def _sigmoid(x):
    return 0.5 * jnp.tanh(0.5 * x) + 0.5


def _silu(x):
    hx = 0.5 * x
    return hx + hx * jnp.tanh(hx)


def _rms(x, g):
    return x * lax.rsqrt(jnp.mean(x * x, axis=-1, keepdims=True) + EPS) * g


def _block_rows(n):
    rows = -(-n // SUBLANES)
    while rows % 8 == 0:
        rows += 1
    return rows


def _conv_geometry(nb, seg):
    unit_rows = seg // 2 if nb == 1 else seg
    stride = _block_rows(unit_rows)
    return unit_rows, stride, SUBLANES * stride - unit_rows


def _conv_outputs(slab, row0, stride, i0, n, lanes, convw_ref, convb_ref):
    first = HIST_PAD - HIST
    taps = [jnp.broadcast_to(convw_ref[t:t + 1, lanes], (SUBLANES, LANES)) for t in range(CONV_WIDTH)]
    accs = [jnp.broadcast_to(convb_ref[:, lanes], (SUBLANES, LANES))] * n
    for v in range(i0 + first, i0 + n + first + HIST):
        x = slab[pl.ds(row0 + v, SUBLANES, stride=stride), :]
        for k in range(n):
            t = v - (i0 + k) - first
            if 0 <= t < CONV_WIDTH:
                accs[k] = accs[k] + x * taps[t]
    return accs


def _proj_body(nb, seg, kr_transposed, zero_ref, x_ref, hist_ref, csq_ref, snq_ref, csk_ref, snk_ref, gpre_ref, win_ref,
               convw_ref, convb_ref, lng_ref, lnb_ref, gqa_ref, wq_ref, gkva_ref, wkv_ref, eone_ref,
               q_ref, k_ref, v_ref, yc_ref, ga_ref, ckv_ref, kr_ref, nh_ref, gbuf, gcbuf, hbuf, ybuf):
    rows = nb * seg
    si = pl.program_id(1)

    zero = zero_ref[0]
    hrow0 = pl.multiple_of(zero, BF16_ROWS)
    grow0 = pl.multiple_of(zero, SUBLANES)

    def proj(r0, r1):
        return lax.dot_general(hbuf[pl.ds(hrow0, rows), :], win_ref[r0:r1, :], NT, preferred_element_type=f32)

    n_slabs = D_CONV // LANES
    slab_lanes = [slice(c * LANES, (c + 1) * LANES) for c in range(n_slabs)]
    unit_rows, stride, pad = _conv_geometry(nb, seg)
    units = [(0, 0), (0, unit_rows)] if nb == 1 else [(j, 0) for j in range(nb)]
    spare_g = HIST_PAD + seg + pad
    spare_c = spare_g + SUBLANES

    @pl.when(si == 0)
    def _():
        for c in range(n_slabs):
            gbuf[:, c, 0:HIST_PAD, :] = hist_ref[:, :, slab_lanes[c]]
            for j, base in units if pad else ():
                end = HIST_PAD + base + unit_rows
                gbuf[j, c, end:end + pad, :] = jnp.zeros((pad, LANES), f32)

    half = rows // 2

    def glu_half(part):
        x = (x_ref[0, part * half:(part + 1) * half, :] if nb == 1
             else x_ref[part * (nb // 2):(part + 1) * (nb // 2)].reshape(half, D_MODEL))
        h = _rms(x, gpre_ref[...]).astype(bf16)
        hbuf[part * half:(part + 1) * half, :] = h
        ab = lax.dot_general(h, win_ref[R_A:R_GC, :], NT, preferred_element_type=f32)
        glu = ab[:, :D_CONV] * _sigmoid(ab[:, D_CONV:])
        for c in range(n_slabs):
            if nb == 1:
                gbuf[0, c, HIST_PAD + part * half:HIST_PAD + (part + 1) * half, :] = glu[:, slab_lanes[c]]
            else:
                gbuf[part * (nb // 2):(part + 1) * (nb // 2), c, HIST_PAD:HIST_PAD + seg, :] = (
                    glu[:, slab_lanes[c]].reshape(nb // 2, seg, LANES))
        return ab

    glu_half(0)

    def conv_rows(j, base, i0, n):
        accs = []
        for c in range(n_slabs):
            accs.append(_conv_outputs(gbuf.at[j, c], grow0 + base, stride, i0, n, slab_lanes[c], convw_ref, convb_ref))
            if c + 1 < n_slabs:
                gbuf[0, 0, spare_c:spare_c + SUBLANES, :] = accs[-1][-1]
        y = jnp.concatenate([jnp.concatenate([accs[c][k] for c in range(n_slabs)], axis=1) for k in range(n)], axis=0)
        mu = jnp.mean(y, axis=-1, keepdims=True)
        d = y - mu
        var = jnp.mean(d * d, axis=-1, keepdims=True)
        yn = d * lax.rsqrt(var + EPS) * lng_ref[...] + lnb_ref[...]
        ys = _silu(yn)
        for k in range(n):
            for c in range(n_slabs):
                ybuf[j, c, pl.ds(base + i0 + k, SUBLANES, stride=stride), :] = (
                    ys[k * SUBLANES:(k + 1) * SUBLANES, slab_lanes[c]])
        return jnp.concatenate([ys[0:SUBLANES, 0:LANES]] * 2, axis=0).astype(bf16)

    t0 = pl.multiple_of(si * seg, seg)
    rows_of = lambda ref: ref[pl.ds(t0, seg), :]
    carry = {}

    def gate_c():
        gc = proj(R_GC, R_QC)
        gcbuf[...] = _silu(gc).reshape(nb, seg, D_CONV)
        return gc

    def gate_a():
        ga = proj(R_GA, R_END)
        ga_ref[...] = _silu(ga).astype(bf16).reshape(nb, seg, D_ATT)
        return ga

    def q_latent():
        qc = proj(R_QC, R_KVC)
        carry["qn"] = _rms(qc, gqa_ref[...]).astype(bf16)
        return qc

    def q_heads():
        qq = jnp.dot(carry["qn"], wq_ref[...], preferred_element_type=f32)
        csq, snq = rows_of(csq_ref), rows_of(snq_ref)
        for hh in range(N_HEADS):
            lo = hh * LANES
            qa = qq[:, lo:lo + LANES].reshape(nb, seg, LANES)
            qb = qq[:, HEAD_W + lo:HEAD_W + lo + LANES].reshape(nb, seg, LANES)
            q_ref[:, :, lo:lo + LANES] = (qa * csq + qb * snq).astype(bf16)
        return qq

    def k_rope():
        kr = proj(R_KR, R_KR + LANES)
        lane = lax.broadcasted_iota(jnp.int32, (rows, LANES), 1)
        kr_rot = jnp.where(lane < HALF, pltpu.roll(kr, LANES - HALF, axis=1), pltpu.roll(kr, HALF, axis=1))
        krope = kr.reshape(nb, seg, LANES) * rows_of(csk_ref) + kr_rot.reshape(nb, seg, LANES) * rows_of(snk_ref)
        if kr_transposed:
            kr_ref[0] = krope[0].T[0:QK_ROPE, :]
        else:
            kr_ref[...] = krope[:, :, 0:QK_ROPE]
        carry["krope_k"] = pltpu.roll(krope.reshape(rows, LANES), ROPE_LO, axis=1)
        return kr

    def kv_heads():
        ckv = _rms(proj(R_KVC, R_KR), gkva_ref[...])
        ckv_ref[...] = ckv.reshape(nb, seg, KV_LORA)
        kk = jnp.dot(ckv.astype(bf16), wkv_ref[...], preferred_element_type=f32)
        is_k = lax.broadcasted_iota(jnp.int32, (rows, LANES), 1) < QK_NOPE
        for hh in range(N_HEADS):
            kv = kk[:, hh * LANES:(hh + 1) * LANES]
            k_ref[:, :, hh * LANES:(hh + 1) * LANES] = (
                jnp.where(is_k, kv, carry["krope_k"]).astype(bf16).reshape(nb, seg, LANES))
            v_ref[:, :, hh * LANES:(hh + 1) * LANES] = (
                jnp.where(is_k, eone_ref[...], kv).astype(bf16).reshape(nb, seg, LANES))
        return kk

    stages = [functools.partial(glu_half, 1), gate_c, gate_a, q_latent, q_heads, k_rope, kv_heads]
    per_unit = max(1, -(-len(stages) // len(units)))
    splits = [(stride * k) // per_unit for k in range(per_unit + 1)]
    chunks = [(j, base, splits[k], splits[k + 1] - splits[k]) for j, base in units for k in range(per_unit)]
    assert len(chunks) >= len(stages)
    pending = None
    for n, chunk in enumerate(chunks):
        conv_piece = conv_rows(*chunk)
        if pending is not None:
            gbuf[0, 0, spare_g:spare_g + SUBLANES, :] = pending[0:SUBLANES, 0:LANES]
            pending = None
        if n < len(stages):
            pending = stages[n]()
        hbuf[rows:rows + BF16_ROWS, 0:LANES] = conv_piece
    ys_all = jnp.concatenate([ybuf[:, c, 0:seg, :] for c in range(n_slabs)], axis=2)
    yc_ref[...] = (ys_all * gcbuf[...]).astype(bf16)

    for c in range(n_slabs):
        tail = gbuf[:, c, seg:seg + HIST_PAD, :]
        nh_ref[:, :, slab_lanes[c]] = tail
        gbuf[:, c, 0:HIST_PAD, :] = tail


def _proj_call(x, hist, tables, w, nb, seg, kr_transposed):
    B, T, _ = x.shape
    grid = (B // nb, T // seg)
    pad = _conv_geometry(nb, seg)[2]
    full = lambda a: pl.BlockSpec(a.shape, lambda b, s: (0,) * a.ndim)
    tok = lambda width: pl.BlockSpec((nb, seg, width), lambda b, s: (b, s, 0))
    weights = (w["g_pre"], w["w_in_t"], w["conv_w"], w["conv_b"], w["ln_g"], w["ln_b"],
               w["g_qa"], w["w_q"], w["g_kva"], w["w_kv"], w["e_one"])
    if kr_transposed:
        assert nb == 1
        kr_shape, kr_spec = (B, QK_ROPE, T), pl.BlockSpec((1, QK_ROPE, seg), lambda b, s: (b, 0, s))
    else:
        kr_shape, kr_spec = (B, T, QK_ROPE), tok(QK_ROPE)
    out_shape = (
        jax.ShapeDtypeStruct((B, T, HEAD_W), bf16),
        jax.ShapeDtypeStruct((B, T, HEAD_W), bf16),
        jax.ShapeDtypeStruct((B, T, HEAD_W), bf16),
        jax.ShapeDtypeStruct((B, T, D_CONV), bf16),
        jax.ShapeDtypeStruct((B, T, D_ATT), bf16),
        jax.ShapeDtypeStruct((B, T, KV_LORA), f32),
        jax.ShapeDtypeStruct(kr_shape, f32),
        jax.ShapeDtypeStruct((B, HIST_PAD, D_CONV), f32),
    )
    out_specs = (tok(HEAD_W), tok(HEAD_W), tok(HEAD_W), tok(D_CONV), tok(D_ATT), tok(KV_LORA), kr_spec,
                 pl.BlockSpec((nb, HIST_PAD, D_CONV), lambda b, s: (b, 0, 0)))
    return pl.pallas_call(
        functools.partial(_proj_body, nb, seg, kr_transposed),
        grid=grid,
        in_specs=[pl.BlockSpec(memory_space=pltpu.SMEM), tok(D_MODEL),
                  pl.BlockSpec((nb, HIST_PAD, D_CONV), lambda b, s: (b, 0, 0))]
        + [full(t) for t in tables] + [full(a) for a in weights],
        out_specs=out_specs,
        out_shape=out_shape,
        scratch_shapes=[pltpu.VMEM((nb, D_CONV // LANES, HIST_PAD + seg + pad + 2 * SUBLANES, LANES), f32),
                        pltpu.VMEM((nb, seg, D_CONV), f32),
                        pltpu.VMEM((nb * seg + BF16_ROWS, D_MODEL), bf16),
                        pltpu.VMEM((nb, D_CONV // LANES, seg + pad, LANES), f32)],
        compiler_params=pltpu.CompilerParams(dimension_semantics=("arbitrary", "arbitrary"),
                                             vmem_limit_bytes=VMEM_LIMIT),
        name="proj",
    )(jnp.zeros((1,), jnp.int32), x, hist, *tables, *weights)


def _softmax_pv(parts):
    m = None
    for s, _ in parts:
        mi = jnp.max(s, axis=-1, keepdims=True)
        m = mi if m is None else jnp.maximum(m, mi)
    o = None
    for s, v in parts:
        oi = jnp.dot(jnp.exp(s - m).astype(bf16), v, preferred_element_type=f32)
        o = oi if o is None else o + oi
    return o


def _attend_prompt_body(seq, q_ref, k_ref, v_ref, ga_ref, o_ref):
    tq = ATT_TQ
    row_chunk = lax.broadcasted_iota(jnp.int32, (tq, tq), 0) // CHUNK
    col_chunk = lax.broadcasted_iota(jnp.int32, (tq, tq), 1) // CHUNK
    visible = col_chunk <= row_chunk
    lane = lax.broadcasted_iota(jnp.int32, (tq, LANES), 1)
    def scores(qi, e):
        q0 = qi * tq
        lanes = slice(e * LANES, (e + 1) * LANES)
        qh = q_ref[0, q0:q0 + tq, lanes]
        s_diag = lax.dot_general(qh, k_ref[0, q0:q0 + tq, lanes], NT, preferred_element_type=f32)
        parts = [(jnp.where(visible, s_diag, NEG), v_ref[0, q0:q0 + tq, lanes])]
        if qi > 0:
            s_off = lax.dot_general(qh, k_ref[0, 0:q0, lanes], NT, preferred_element_type=f32)
            parts.append((s_off, v_ref[0, 0:q0, lanes]))
        return parts

    items = [(qi, e) for qi in range(seq // tq) for e in range(ATT_HEADS)]
    ahead = 4
    queue = [scores(*it) for it in items[:ahead]]
    heads = []
    for idx, (qi, e) in enumerate(items):
        cur = queue.pop(0)
        if idx + ahead < len(items):
            queue.append(scores(*items[idx + ahead]))
        o = _softmax_pv(cur)
        heads.append(o / o[:, 0:1])
        if e % 2 == 1:
            q0 = qi * tq
            out_lanes = slice((e // 2) * LANES, (e // 2 + 1) * LANES)
            y = jnp.where(lane < V_DIM, pltpu.roll(heads[0], V_DIM, axis=1), heads[1])
            o_ref[0, q0:q0 + tq, out_lanes] = (y * ga_ref[0, q0:q0 + tq, out_lanes].astype(f32)).astype(bf16)
            heads = []


def _attend_prompt_call(q, k, v, ga):
    B, S, _ = q.shape
    group = pl.BlockSpec((1, S, ATT_HEADS * LANES), lambda b, p: (b, 0, p))
    half = pl.BlockSpec((1, S, ATT_HEADS * V_DIM), lambda b, p: (b, 0, p))
    return pl.pallas_call(
        functools.partial(_attend_prompt_body, S),
        grid=(B, N_HEADS // ATT_HEADS),
        in_specs=[group, group, group, half],
        out_specs=half,
        out_shape=jax.ShapeDtypeStruct((B, S, D_ATT), bf16),
        compiler_params=pltpu.CompilerParams(dimension_semantics=("arbitrary", "arbitrary"),
                                             vmem_limit_bytes=VMEM_LIMIT),
        name="attend_prompt",
    )(q, k, v, ga)


def _attend_sample_body(q_ref, ckvn_ref, krn_ref, ckvp_ref, krpt_ref, wabs_ref, wv_ref, ga_ref, o_ref):
    g_seqs, t, _ = q_ref.shape
    q = q_ref[...].reshape(g_seqs * t, HEAD_W)
    q_cat = [jnp.dot(q[:, hh * LANES:(hh + 1) * LANES], wabs_ref[hh], preferred_element_type=f32).astype(bf16)
             for hh in range(N_HEADS)]

    def scores(g):
        qg = jnp.concatenate([qc[g * t:(g + 1) * t] for qc in q_cat], axis=0)
        q_lat = qg[:, :KV_LORA]
        q_rope = qg[:, KV_LORA:KV_LORA + QK_ROPE]
        ckv_p = ckvp_ref[g].astype(bf16)
        ckv_n = ckvn_ref[g].astype(bf16)
        s_p = (lax.dot_general(q_lat, ckv_p, NT, preferred_element_type=f32)
               + jnp.dot(q_rope, krpt_ref[g].astype(bf16), preferred_element_type=f32))
        s_n = (lax.dot_general(q_lat, ckv_n, NT, preferred_element_type=f32)
               + lax.dot_general(q_rope, krn_ref[g].astype(bf16), NT, preferred_element_type=f32))
        return s_p, s_n, ckv_p, ckv_n

    def latent_out(s_p, s_n, ckv_p, ckv_n):
        m = jnp.maximum(jnp.max(s_p, axis=-1, keepdims=True), jnp.max(s_n, axis=-1, keepdims=True))
        p_p = jnp.exp(s_p - m).astype(bf16)
        p_n = jnp.exp(s_n - m).astype(bf16)
        l = jnp.sum(p_p.astype(f32), axis=-1, keepdims=True) + jnp.sum(p_n.astype(f32), axis=-1, keepdims=True)
        o_lat = (jnp.dot(p_p, ckv_p, preferred_element_type=f32) + jnp.dot(p_n, ckv_n, preferred_element_type=f32)) / l
        return o_lat.astype(bf16)

    nxt = scores(0)
    o_lat = []
    for g in range(g_seqs):
        cur = nxt
        if g + 1 < g_seqs:
            nxt = scores(g + 1)
        o_lat.append(latent_out(*cur))
    y = None
    for hh in range(N_HEADS):
        o_h = jnp.concatenate([o[hh * t:(hh + 1) * t] for o in o_lat], axis=0)
        yh = jnp.dot(o_h, wv_ref[hh], preferred_element_type=f32)
        y = yh if y is None else y + yh
    o_ref[...] = (y.reshape(g_seqs, t, D_ATT) * ga_ref[...].astype(f32)).astype(bf16)


def _attend_sample_call(q, ckv_new, kr_new, ckv_past, kr_past_t, w, ga):
    B, T, _ = q.shape
    P = ckv_past.shape[1]
    per_b = lambda shape: pl.BlockSpec((SAMPLE_ATT_GROUP,) + shape, lambda b: (b, 0, 0))
    full = lambda a: pl.BlockSpec(a.shape, lambda b: (0,) * a.ndim)
    return pl.pallas_call(
        _attend_sample_body,
        grid=(B // SAMPLE_ATT_GROUP,),
        in_specs=[per_b((T, HEAD_W)), per_b((T, KV_LORA)), per_b((T, QK_ROPE)),
                  per_b((P, KV_LORA)), per_b((QK_ROPE, P)), full(w["w_abs"]), full(w["w_vh"]),
                  per_b((T, D_ATT))],
        out_specs=per_b((T, D_ATT)),
        out_shape=jax.ShapeDtypeStruct((B, T, D_ATT), bf16),
        compiler_params=pltpu.CompilerParams(dimension_semantics=("arbitrary",),
                                             vmem_limit_bytes=VMEM_LIMIT),
        name="attend_sample",
    )(q, ckv_new, kr_new, ckv_past, kr_past_t, w["w_abs"], w["w_vh"], ga)


def _out_body(yc_ref, ya_ref, x_ref, wout_ref, gpost_ref, y_ref):
    rows = y_ref.shape[0]
    sub = min(OUT_SUB_ROWS, rows)

    def mix(i):
        r = slice(i * sub, (i + 1) * sub)
        return (jnp.dot(yc_ref[r, :], wout_ref[0:D_CONV, :], preferred_element_type=f32)
                + jnp.dot(ya_ref[r, :], wout_ref[D_CONV:, :], preferred_element_type=f32))

    nxt = mix(0)
    for i in range(rows // sub):
        cur = nxt
        if i + 1 < rows // sub:
            nxt = mix(i + 1)
        r = slice(i * sub, (i + 1) * sub)
        y_ref[r, :] = x_ref[r, :] + _rms(cur, gpost_ref[...])


def _out_call(yc, ya, x, w):
    n = x.shape[0]
    rows = min(OUT_ROWS, n)
    tok = lambda width: pl.BlockSpec((rows, width), lambda i: (i, 0))
    full = lambda a: pl.BlockSpec(a.shape, lambda i: (0,) * a.ndim)
    return pl.pallas_call(
        _out_body,
        grid=(n // rows,),
        in_specs=[tok(D_CONV), tok(D_ATT), tok(D_MODEL), full(w["w_out"]), full(w["g_post"])],
        out_specs=tok(D_MODEL),
        out_shape=jax.ShapeDtypeStruct((n, D_MODEL), f32),
        compiler_params=pltpu.CompilerParams(dimension_semantics=("arbitrary",),
                                             vmem_limit_bytes=VMEM_LIMIT),
        name="out",
    )(yc, ya, x, w["w_out"], w["g_post"])


def _rot_cols(rope):
    return jnp.concatenate([-rope[..., HALF:], rope[..., :HALF]], axis=-1)


def _layout_weights(g_pre, w_in, conv_w, conv_b, ln_g, ln_b, g_qa, w_qb, g_kva, w_kvb, w_out, g_post):
    dq = QK_NOPE + QK_ROPE
    pad_q = ((0, 0), (0, 0), (0, LANES - dq))
    wq3 = w_qb.reshape(Q_LORA, N_HEADS, dq)
    w_q_plain = jnp.pad(wq3, pad_q).reshape(Q_LORA, HEAD_W)
    w_q_rot = jnp.pad(_rot_cols(wq3[:, :, QK_NOPE:]), ((0, 0), (0, 0), (ROPE_LO, LANES - dq))).reshape(Q_LORA, HEAD_W)
    w_q = jnp.concatenate([w_q_plain, w_q_rot], axis=1)

    w_kv = w_kvb
    e_one = (jnp.arange(LANES) == 0).astype(f32)[None]

    wkv3 = w_kvb.reshape(KV_LORA, N_HEADS, QK_NOPE + V_DIM)
    wk_t = jnp.transpose(wkv3[:, :, :QK_NOPE], (1, 2, 0))
    wk_t = jnp.pad(wk_t, ((0, 0), (0, LANES - QK_NOPE), (0, 0)))
    sel = np.zeros((LANES, LANES), np.float32)
    sel[ROPE_LO + np.arange(QK_ROPE), np.arange(QK_ROPE)] = 1.0
    w_abs = jnp.concatenate([wk_t, jnp.broadcast_to(jnp.asarray(sel), (N_HEADS, LANES, LANES))], axis=2)
    wv = jnp.transpose(wkv3[:, :, QK_NOPE:], (1, 0, 2))
    w_vh = (wv[:, :, None, :] * jnp.eye(N_HEADS, dtype=f32)[:, None, :, None]).reshape(N_HEADS, KV_LORA, D_ATT)

    row = lambda v: v.reshape(1, -1)
    return {
        "g_pre": row(g_pre), "w_in_t": jnp.swapaxes(w_in, 0, 1).astype(bf16), "conv_w": conv_w,
        "conv_b": row(conv_b), "ln_g": row(ln_g), "ln_b": row(ln_b), "g_qa": row(g_qa), "w_q": w_q.astype(bf16),
        "g_kva": row(g_kva), "w_kv": w_kv.astype(bf16), "e_one": e_one,
        "w_abs": w_abs.astype(bf16), "w_vh": w_vh.astype(bf16),
        "w_out": w_out.astype(bf16), "g_post": row(g_post),
    }


def _rope_tables(start, length):
    inv = ROPE_THETA ** (-np.arange(0, QK_ROPE, 2, dtype=np.float64) / QK_ROPE)
    ang = np.arange(start, start + length, dtype=np.float64)[:, None] * inv[None, :]
    cos, sin = np.cos(ang), np.sin(ang)
    z = lambda n: np.zeros((length, n))
    tail = LANES - QK_NOPE - QK_ROPE
    cs_q = np.concatenate([np.full((length, QK_NOPE), ATT_SCALE), cos * ATT_SCALE, cos * ATT_SCALE, z(tail)], axis=1)
    sn_q = np.concatenate([z(QK_NOPE), sin * ATT_SCALE, sin * ATT_SCALE, z(tail)], axis=1)
    cs_k = np.concatenate([cos, cos, z(LANES - QK_ROPE)], axis=1)
    sn_k = np.concatenate([-sin, sin, z(LANES - QK_ROPE)], axis=1)
    return tuple(jnp.asarray(t, dtype=f32) for t in (cs_q, sn_q, cs_k, sn_k))


def _layer(x_prompt, x_sample, ckv_past, kr_past, conv_state, w):
    B, S, _ = x_prompt.shape
    Bs, T, _ = x_sample.shape
    past = ckv_past.shape[1]
    pad_hist = lambda hst: jnp.pad(hst, ((0, 0), (HIST_PAD - HIST, 0), (0, 0)))

    qp, kp, vp, ycp, gap, ckvp, krp_t, nhp = _proj_call(
        x_prompt, jnp.zeros((B, HIST_PAD, D_CONV), f32), _rope_tables(0, S), w, 1, PROJ_ROWS, True)
    yap = _attend_prompt_call(qp, kp, vp, gap)
    y_prompt = _out_call(ycp.reshape(B * S, D_CONV), yap.reshape(B * S, D_ATT),
                         x_prompt.reshape(B * S, D_MODEL), w).reshape(B, S, D_MODEL)

    qs, _, _, ycs, gas, ckvs, krs, nhs = _proj_call(
        x_sample, pad_hist(conv_state), _rope_tables(past, T), w, SAMPLE_GROUP, T, False)
    yas = _attend_sample_call(qs, ckvs, krs, ckv_past, jnp.swapaxes(kr_past, 1, 2), w, gas)
    y_sample = _out_call(ycs.reshape(Bs * T, D_CONV), yas.reshape(Bs * T, D_ATT),
                         x_sample.reshape(Bs * T, D_MODEL), w).reshape(Bs, T, D_MODEL)

    trim = lambda nh: nh[:, HIST_PAD - HIST:, :]
    return y_prompt, y_sample, ckvp, jnp.swapaxes(krp_t, 1, 2), trim(nhp), ckvs, krs, trim(nhs)


def kernel(x_prompt, x_sample, cache_ckv, cache_krope, state_conv, g_pre, w_in, conv_w, conv_b, conv_ln_g,
           conv_ln_b, g_qa, w_qb, g_kva, w_kvb, w_out, g_post):
    depth = w_in.shape[0]
    yp, ys = x_prompt, x_sample
    outs = [[] for _ in range(6)]
    for l in range(depth):
        w = _layout_weights(g_pre[l], w_in[l], conv_w[l], conv_b[l], conv_ln_g[l], conv_ln_b[l], g_qa[l],
                            w_qb[l], g_kva[l], w_kvb[l], w_out[l], g_post[l])
        yp, ys, *caches = _layer(yp, ys, cache_ckv[l], cache_krope[l], state_conv[l], w)
        for dst, val in zip(outs, caches):
            dst.append(val)
    return (yp, ys) + tuple(jnp.stack(o) for o in outs)
```

```python
import functools

import jax
import jax.numpy as jnp
import numpy as np
from jax import lax
from jax.experimental import pallas as pl
from jax.experimental.pallas import tpu as pltpu

D_MODEL = 1024
CHUNK = 64
D_CONV = 512
D_ATT = 512
CONV_WIDTH = 31
HIST = CONV_WIDTH - 1
N_HEADS = 8
QK_NOPE = 64
QK_ROPE = 32
V_DIM = 64
Q_LORA = 256
KV_LORA = 128
ROPE_THETA = 10000.0
EPS = 1e-6
ATT_SCALE = (QK_NOPE + QK_ROPE) ** -0.5
Q_SCALE = ATT_SCALE * float(np.log2(np.e))
NEG = -1e30

LANES = 128
SUBLANES = 8
BF16_ROWS = 16
HIST_PAD = 32
HEAD_W = N_HEADS * LANES
ROPE_LO = QK_NOPE
HALF = QK_ROPE // 2
R_A, R_GC, R_QC, R_KVC, R_KR, R_GA, R_END = 0, 1024, 1536, 1792, 1920, 1952, 2464

PROJ_ROWS = 512
ATT_TQ = 256
ATT_HEADS = 2
OUT_ROWS = 1024
OUT_SUB_ROWS = 256
SAMPLE_GROUP = 8
SAMPLE_ATT_GROUP = 4
VMEM_LIMIT = 56 * 1024 * 1024

f32 = jnp.float32
bf16 = jnp.bfloat16
NT = (((1,), (1,)), ((), ()))


def _sigmoid(x):
    return 0.5 * jnp.tanh(0.5 * x) + 0.5


def _silu(x):
    hx = 0.5 * x
    return hx + hx * jnp.tanh(hx)


def _rms(x, g):
    return x * lax.rsqrt(jnp.mean(x * x, axis=-1, keepdims=True) + EPS) * g


def _block_rows(n):
    rows = -(-n // SUBLANES)
    while rows % 8 == 0:
        rows += 1
    return rows


def _conv_geometry(nb, seg):
    unit_rows = seg // 2 if nb == 1 else seg
    stride = _block_rows(unit_rows)
    return unit_rows, stride, SUBLANES * stride - unit_rows


def _conv_outputs(slab, row0, stride, i0, n, lanes, convw_ref, convb_ref):
    first = HIST_PAD - HIST
    taps = [jnp.broadcast_to(convw_ref[t:t + 1, lanes], (SUBLANES, LANES)) for t in range(CONV_WIDTH)]
    accs = [jnp.broadcast_to(convb_ref[:, lanes], (SUBLANES, LANES))] * n
    for v in range(i0 + first, i0 + n + first + HIST):
        x = slab[pl.ds(row0 + v, SUBLANES, stride=stride), :]
        for k in range(n):
            t = v - (i0 + k) - first
            if 0 <= t < CONV_WIDTH:
                accs[k] = accs[k] + x * taps[t]
    return accs


def _proj_body(nb, seg, kr_transposed, zero_ref, x_ref, hist_ref, csq_ref, snq_ref, csk_ref, snk_ref, gpre_ref, win_ref,
               convw_ref, convb_ref, lng_ref, lnb_ref, gqa_ref, wq_ref, gkva_ref, wkv_ref, eone_ref,
               q_ref, k_ref, v_ref, yc_ref, ga_ref, ckv_ref, kr_ref, nh_ref, gbuf, gcbuf, hbuf, ybuf):
    rows = nb * seg
    si = pl.program_id(1)

    zero = zero_ref[0]
    hrow0 = pl.multiple_of(zero, BF16_ROWS)
    grow0 = pl.multiple_of(zero, SUBLANES)

    def proj(r0, r1):
        return lax.dot_general(hbuf[pl.ds(hrow0, rows), :], win_ref[r0:r1, :], NT, preferred_element_type=f32)

    n_slabs = D_CONV // LANES
    slab_lanes = [slice(c * LANES, (c + 1) * LANES) for c in range(n_slabs)]
    unit_rows, stride, pad = _conv_geometry(nb, seg)
    units = [(0, 0), (0, unit_rows)] if nb == 1 else [(j, 0) for j in range(nb)]
    spare_g = HIST_PAD + seg + pad
    spare_c = spare_g + SUBLANES

    @pl.when(si == 0)
    def _():
        for c in range(n_slabs):
            gbuf[:, c, 0:HIST_PAD, :] = hist_ref[:, :, slab_lanes[c]]
            for j, base in units if pad else ():
                end = HIST_PAD + base + unit_rows
                gbuf[j, c, end:end + pad, :] = jnp.zeros((pad, LANES), f32)

    half = rows // 2

    def glu_half(part):
        x = (x_ref[0, part * half:(part + 1) * half, :] if nb == 1
             else x_ref[part * (nb // 2):(part + 1) * (nb // 2)].reshape(half, D_MODEL))
        h = _rms(x, gpre_ref[...]).astype(bf16)
        hbuf[part * half:(part + 1) * half, :] = h
        ab = lax.dot_general(h, win_ref[R_A:R_GC, :], NT, preferred_element_type=f32)
        glu = ab[:, :D_CONV] * _sigmoid(ab[:, D_CONV:])
        for c in range(n_slabs):
            if nb == 1:
                gbuf[0, c, HIST_PAD + part * half:HIST_PAD + (part + 1) * half, :] = glu[:, slab_lanes[c]]
            else:
                gbuf[part * (nb // 2):(part + 1) * (nb // 2), c, HIST_PAD:HIST_PAD + seg, :] = (
                    glu[:, slab_lanes[c]].reshape(nb // 2, seg, LANES))
        return ab

    glu_half(0)

    def conv_rows(j, base, i0, n):
        accs = []
        for c in range(n_slabs):
            accs.append(_conv_outputs(gbuf.at[j, c], grow0 + base, stride, i0, n, slab_lanes[c], convw_ref, convb_ref))
            if c + 1 < n_slabs:
                gbuf[0, 0, spare_c:spare_c + SUBLANES, :] = accs[-1][-1]
        y = jnp.concatenate([jnp.concatenate([accs[c][k] for c in range(n_slabs)], axis=1) for k in range(n)], axis=0)
        mu = jnp.mean(y, axis=-1, keepdims=True)
        d = y - mu
        var = jnp.mean(d * d, axis=-1, keepdims=True)
        yn = d * lax.rsqrt(var + EPS) * lng_ref[...] + lnb_ref[...]
        ys = _silu(yn)
        for k in range(n):
            for c in range(n_slabs):
                ybuf[j, c, pl.ds(base + i0 + k, SUBLANES, stride=stride), :] = (
                    ys[k * SUBLANES:(k + 1) * SUBLANES, slab_lanes[c]])
        return jnp.concatenate([ys[0:SUBLANES, 0:LANES]] * 2, axis=0).astype(bf16)

    t0 = pl.multiple_of(si * seg, seg)
    rows_of = lambda ref: ref[pl.ds(t0, seg), :]
    carry = {}

    def gate_c():
        gc = proj(R_GC, R_QC)
        gcbuf[...] = _silu(gc).reshape(nb, seg, D_CONV)
        return gc

    def gate_a():
        ga = proj(R_GA, R_END)
        ga_ref[...] = _silu(ga).astype(bf16).reshape(nb, seg, D_ATT)
        return ga

    def q_latent():
        qc = proj(R_QC, R_KVC)
        carry["qn"] = _rms(qc, gqa_ref[...]).astype(bf16)
        return qc

    def q_heads():
        qq = jnp.dot(carry["qn"], wq_ref[...], preferred_element_type=f32)
        csq, snq = rows_of(csq_ref), rows_of(snq_ref)
        for hh in range(N_HEADS):
            lo = hh * LANES
            qa = qq[:, lo:lo + LANES].reshape(nb, seg, LANES)
            qb = qq[:, HEAD_W + lo:HEAD_W + lo + LANES].reshape(nb, seg, LANES)
            q_ref[:, :, lo:lo + LANES] = (qa * csq + qb * snq).astype(bf16)
        return qq

    def k_rope():
        kr = proj(R_KR, R_KR + LANES)
        lane = lax.broadcasted_iota(jnp.int32, (rows, LANES), 1)
        kr_rot = jnp.where(lane < HALF, pltpu.roll(kr, LANES - HALF, axis=1), pltpu.roll(kr, HALF, axis=1))
        krope = kr.reshape(nb, seg, LANES) * rows_of(csk_ref) + kr_rot.reshape(nb, seg, LANES) * rows_of(snk_ref)
        if kr_transposed:
            kr_ref[0] = krope[0].T[0:QK_ROPE, :]
        else:
            kr_ref[...] = krope[:, :, 0:QK_ROPE]
        carry["krope_k"] = pltpu.roll(krope.reshape(rows, LANES), ROPE_LO, axis=1)
        return kr

    def kv_heads():
        ckv = _rms(proj(R_KVC, R_KR), gkva_ref[...])
        ckv_ref[...] = ckv.reshape(nb, seg, KV_LORA)
        kk = jnp.dot(ckv.astype(bf16), wkv_ref[...], preferred_element_type=f32)
        is_k = lax.broadcasted_iota(jnp.int32, (rows, LANES), 1) < QK_NOPE
        for hh in range(N_HEADS):
            kv = kk[:, hh * LANES:(hh + 1) * LANES]
            k_ref[:, :, hh * LANES:(hh + 1) * LANES] = (
                jnp.where(is_k, kv, carry["krope_k"]).astype(bf16).reshape(nb, seg, LANES))
            v_ref[:, :, hh * LANES:(hh + 1) * LANES] = (
                jnp.where(is_k, eone_ref[...], kv).astype(bf16).reshape(nb, seg, LANES))
        return kk

    stages = [functools.partial(glu_half, 1), gate_c, gate_a, q_latent, q_heads, k_rope, kv_heads]
    per_unit = max(1, -(-len(stages) // len(units)))
    splits = [(stride * k) // per_unit for k in range(per_unit + 1)]
    chunks = [(j, base, splits[k], splits[k + 1] - splits[k]) for j, base in units for k in range(per_unit)]
    assert len(chunks) >= len(stages)
    pending = None
    for n, chunk in enumerate(chunks):
        conv_piece = conv_rows(*chunk)
        if pending is not None:
            gbuf[0, 0, spare_g:spare_g + SUBLANES, :] = pending[0:SUBLANES, 0:LANES]
            pending = None
        if n < len(stages):
            pending = stages[n]()
        hbuf[rows:rows + BF16_ROWS, 0:LANES] = conv_piece
    ys_all = jnp.concatenate([ybuf[:, c, 0:seg, :] for c in range(n_slabs)], axis=2)
    yc_ref[...] = (ys_all * gcbuf[...]).astype(bf16)

    for c in range(n_slabs):
        tail = gbuf[:, c, seg:seg + HIST_PAD, :]
        nh_ref[:, :, slab_lanes[c]] = tail
        gbuf[:, c, 0:HIST_PAD, :] = tail


def _proj_call(x, hist, tables, w, nb, seg, kr_transposed):
    B, T, _ = x.shape
    grid = (B // nb, T // seg)
    pad = _conv_geometry(nb, seg)[2]
    full = lambda a: pl.BlockSpec(a.shape, lambda b, s: (0,) * a.ndim)
    tok = lambda width: pl.BlockSpec((nb, seg, width), lambda b, s: (b, s, 0))
    weights = (w["g_pre"], w["w_in_t"], w["conv_w"], w["conv_b"], w["ln_g"], w["ln_b"],
               w["g_qa"], w["w_q"], w["g_kva"], w["w_kv"], w["e_one"])
    if kr_transposed:
        assert nb == 1
        kr_shape, kr_spec = (B, QK_ROPE, T), pl.BlockSpec((1, QK_ROPE, seg), lambda b, s: (b, 0, s))
    else:
        kr_shape, kr_spec = (B, T, QK_ROPE), tok(QK_ROPE)
    out_shape = (
        jax.ShapeDtypeStruct((B, T, HEAD_W), bf16),
        jax.ShapeDtypeStruct((B, T, HEAD_W), bf16),
        jax.ShapeDtypeStruct((B, T, HEAD_W), bf16),
        jax.ShapeDtypeStruct((B, T, D_CONV), bf16),
        jax.ShapeDtypeStruct((B, T, D_ATT), bf16),
        jax.ShapeDtypeStruct((B, T, KV_LORA), f32),
        jax.ShapeDtypeStruct(kr_shape, f32),
        jax.ShapeDtypeStruct((B, HIST_PAD, D_CONV), f32),
    )
    out_specs = (tok(HEAD_W), tok(HEAD_W), tok(HEAD_W), tok(D_CONV), tok(D_ATT), tok(KV_LORA), kr_spec,
                 pl.BlockSpec((nb, HIST_PAD, D_CONV), lambda b, s: (b, 0, 0)))
    return pl.pallas_call(
        functools.partial(_proj_body, nb, seg, kr_transposed),
        grid=grid,
        in_specs=[pl.BlockSpec(memory_space=pltpu.SMEM), tok(D_MODEL),
                  pl.BlockSpec((nb, HIST_PAD, D_CONV), lambda b, s: (b, 0, 0))]
        + [full(t) for t in tables] + [full(a) for a in weights],
        out_specs=out_specs,
        out_shape=out_shape,
        scratch_shapes=[pltpu.VMEM((nb, D_CONV // LANES, HIST_PAD + seg + pad + 2 * SUBLANES, LANES), f32),
                        pltpu.VMEM((nb, seg, D_CONV), f32),
                        pltpu.VMEM((nb * seg + BF16_ROWS, D_MODEL), bf16),
                        pltpu.VMEM((nb, D_CONV // LANES, seg + pad, LANES), f32)],
        compiler_params=pltpu.CompilerParams(dimension_semantics=("arbitrary", "arbitrary"),
                                             vmem_limit_bytes=VMEM_LIMIT),
        name="proj",
    )(jnp.zeros((1,), jnp.int32), x, hist, *tables, *weights)


def _softmax_pv(parts):
    m = None
    for s, _ in parts:
        mi = jnp.max(s, axis=-1, keepdims=True)
        m = mi if m is None else jnp.maximum(m, mi)
    o = None
    for s, v in parts:
        oi = jnp.dot(jnp.exp2(s - m).astype(bf16), v, preferred_element_type=f32)
        o = oi if o is None else o + oi
    return o


def _attend_prompt_body(seq, q_ref, k_ref, v_ref, ga_ref, o_ref):
    tq = ATT_TQ
    row_chunk = lax.broadcasted_iota(jnp.int32, (tq, tq), 0) // CHUNK
    col_chunk = lax.broadcasted_iota(jnp.int32, (tq, tq), 1) // CHUNK
    visible = col_chunk <= row_chunk
    lane = lax.broadcasted_iota(jnp.int32, (tq, LANES), 1)
    def scores(qi, e):
        q0 = qi * tq
        lanes = slice(e * LANES, (e + 1) * LANES)
        qh = q_ref[0, q0:q0 + tq, lanes]
        s_diag = lax.dot_general(qh, k_ref[0, q0:q0 + tq, lanes], NT, preferred_element_type=f32)
        parts = [(jnp.where(visible, s_diag, NEG), v_ref[0, q0:q0 + tq, lanes])]
        if qi > 0:
            s_off = lax.dot_general(qh, k_ref[0, 0:q0, lanes], NT, preferred_element_type=f32)
            parts.append((s_off, v_ref[0, 0:q0, lanes]))
        return parts

    items = [(qi, e) for qi in range(seq // tq) for e in range(ATT_HEADS)]
    ahead = 4
    queue = [scores(*it) for it in items[:ahead]]
    heads = []
    for idx, (qi, e) in enumerate(items):
        cur = queue.pop(0)
        if idx + ahead < len(items):
            queue.append(scores(*items[idx + ahead]))
        o = _softmax_pv(cur)
        heads.append(o / o[:, 0:1])
        if e % 2 == 1:
            q0 = qi * tq
            out_lanes = slice((e // 2) * LANES, (e // 2 + 1) * LANES)
            y = jnp.where(lane < V_DIM, pltpu.roll(heads[0], V_DIM, axis=1), heads[1])
            o_ref[0, q0:q0 + tq, out_lanes] = (y * ga_ref[0, q0:q0 + tq, out_lanes].astype(f32)).astype(bf16)
            heads = []


def _attend_prompt_call(q, k, v, ga):
    B, S, _ = q.shape
    group = pl.BlockSpec((1, S, ATT_HEADS * LANES), lambda b, p: (b, 0, p))
    half = pl.BlockSpec((1, S, ATT_HEADS * V_DIM), lambda b, p: (b, 0, p))
    return pl.pallas_call(
        functools.partial(_attend_prompt_body, S),
        grid=(B, N_HEADS // ATT_HEADS),
        in_specs=[group, group, group, half],
        out_specs=half,
        out_shape=jax.ShapeDtypeStruct((B, S, D_ATT), bf16),
        compiler_params=pltpu.CompilerParams(dimension_semantics=("arbitrary", "arbitrary"),
                                             vmem_limit_bytes=VMEM_LIMIT),
        name="attend_prompt",
    )(q, k, v, ga)


def _attend_sample_body(q_ref, ckvn_ref, krn_ref, ckvp_ref, krpt_ref, wabs_ref, wv_ref, ga_ref, o_ref):
    g_seqs, t, _ = q_ref.shape
    q = q_ref[...].reshape(g_seqs * t, HEAD_W)
    q_cat = [jnp.dot(q[:, hh * LANES:(hh + 1) * LANES], wabs_ref[hh], preferred_element_type=f32).astype(bf16)
             for hh in range(N_HEADS)]

    def scores(g):
        qg = jnp.concatenate([qc[g * t:(g + 1) * t] for qc in q_cat], axis=0)
        q_lat = qg[:, :KV_LORA]
        q_rope = qg[:, KV_LORA:KV_LORA + QK_ROPE]
        ckv_p = ckvp_ref[g].astype(bf16)
        ckv_n = ckvn_ref[g].astype(bf16)
        s_p = (lax.dot_general(q_lat, ckv_p, NT, preferred_element_type=f32)
               + jnp.dot(q_rope, krpt_ref[g].astype(bf16), preferred_element_type=f32))
        s_n = (lax.dot_general(q_lat, ckv_n, NT, preferred_element_type=f32)
               + lax.dot_general(q_rope, krn_ref[g].astype(bf16), NT, preferred_element_type=f32))
        return s_p, s_n, ckv_p, ckv_n

    def latent_out(s_p, s_n, ckv_p, ckv_n):
        m = jnp.maximum(jnp.max(s_p, axis=-1, keepdims=True), jnp.max(s_n, axis=-1, keepdims=True))
        p_p = jnp.exp2(s_p - m).astype(bf16)
        p_n = jnp.exp2(s_n - m).astype(bf16)
        l = jnp.sum(p_p.astype(f32), axis=-1, keepdims=True) + jnp.sum(p_n.astype(f32), axis=-1, keepdims=True)
        o_lat = (jnp.dot(p_p, ckv_p, preferred_element_type=f32) + jnp.dot(p_n, ckv_n, preferred_element_type=f32)) / l
        return o_lat.astype(bf16)

    nxt = scores(0)
    o_lat = []
    for g in range(g_seqs):
        cur = nxt
        if g + 1 < g_seqs:
            nxt = scores(g + 1)
        o_lat.append(latent_out(*cur))
    y = None
    for hh in range(N_HEADS):
        o_h = jnp.concatenate([o[hh * t:(hh + 1) * t] for o in o_lat], axis=0)
        yh = jnp.dot(o_h, wv_ref[hh], preferred_element_type=f32)
        y = yh if y is None else y + yh
    o_ref[...] = (y.reshape(g_seqs, t, D_ATT) * ga_ref[...].astype(f32)).astype(bf16)


def _attend_sample_call(q, ckv_new, kr_new, ckv_past, kr_past_t, w, ga):
    B, T, _ = q.shape
    P = ckv_past.shape[1]
    per_b = lambda shape: pl.BlockSpec((SAMPLE_ATT_GROUP,) + shape, lambda b: (b, 0, 0))
    full = lambda a: pl.BlockSpec(a.shape, lambda b: (0,) * a.ndim)
    return pl.pallas_call(
        _attend_sample_body,
        grid=(B // SAMPLE_ATT_GROUP,),
        in_specs=[per_b((T, HEAD_W)), per_b((T, KV_LORA)), per_b((T, QK_ROPE)),
                  per_b((P, KV_LORA)), per_b((QK_ROPE, P)), full(w["w_abs"]), full(w["w_vh"]),
                  per_b((T, D_ATT))],
        out_specs=per_b((T, D_ATT)),
        out_shape=jax.ShapeDtypeStruct((B, T, D_ATT), bf16),
        compiler_params=pltpu.CompilerParams(dimension_semantics=("arbitrary",),
                                             vmem_limit_bytes=VMEM_LIMIT),
        name="attend_sample",
    )(q, ckv_new, kr_new, ckv_past, kr_past_t, w["w_abs"], w["w_vh"], ga)


def _out_body(yc_ref, ya_ref, x_ref, wout_ref, gpost_ref, y_ref):
    rows = y_ref.shape[0]
    sub = min(OUT_SUB_ROWS, rows)

    def mix(i):
        r = slice(i * sub, (i + 1) * sub)
        return (jnp.dot(yc_ref[r, :], wout_ref[0:D_CONV, :], preferred_element_type=f32)
                + jnp.dot(ya_ref[r, :], wout_ref[D_CONV:, :], preferred_element_type=f32))

    nxt = mix(0)
    for i in range(rows // sub):
        cur = nxt
        if i + 1 < rows // sub:
            nxt = mix(i + 1)
        r = slice(i * sub, (i + 1) * sub)
        y_ref[r, :] = x_ref[r, :] + _rms(cur, gpost_ref[...])


def _out_call(yc, ya, x, w):
    n = x.shape[0]
    rows = min(OUT_ROWS, n)
    tok = lambda width: pl.BlockSpec((rows, width), lambda i: (i, 0))
    full = lambda a: pl.BlockSpec(a.shape, lambda i: (0,) * a.ndim)
    return pl.pallas_call(
        _out_body,
        grid=(n // rows,),
        in_specs=[tok(D_CONV), tok(D_ATT), tok(D_MODEL), full(w["w_out"]), full(w["g_post"])],
        out_specs=tok(D_MODEL),
        out_shape=jax.ShapeDtypeStruct((n, D_MODEL), f32),
        compiler_params=pltpu.CompilerParams(dimension_semantics=("arbitrary",),
                                             vmem_limit_bytes=VMEM_LIMIT),
        name="out",
    )(yc, ya, x, w["w_out"], w["g_post"])


def _rot_cols(rope):
    return jnp.concatenate([-rope[..., HALF:], rope[..., :HALF]], axis=-1)


def _layout_weights(g_pre, w_in, conv_w, conv_b, ln_g, ln_b, g_qa, w_qb, g_kva, w_kvb, w_out, g_post):
    dq = QK_NOPE + QK_ROPE
    pad_q = ((0, 0), (0, 0), (0, LANES - dq))
    wq3 = w_qb.reshape(Q_LORA, N_HEADS, dq)
    w_q_plain = jnp.pad(wq3, pad_q).reshape(Q_LORA, HEAD_W)
    w_q_rot = jnp.pad(_rot_cols(wq3[:, :, QK_NOPE:]), ((0, 0), (0, 0), (ROPE_LO, LANES - dq))).reshape(Q_LORA, HEAD_W)
    w_q = jnp.concatenate([w_q_plain, w_q_rot], axis=1)

    w_kv = w_kvb
    e_one = (jnp.arange(LANES) == 0).astype(f32)[None]

    wkv3 = w_kvb.reshape(KV_LORA, N_HEADS, QK_NOPE + V_DIM)
    wk_t = jnp.transpose(wkv3[:, :, :QK_NOPE], (1, 2, 0))
    wk_t = jnp.pad(wk_t, ((0, 0), (0, LANES - QK_NOPE), (0, 0)))
    sel = np.zeros((LANES, LANES), np.float32)
    sel[ROPE_LO + np.arange(QK_ROPE), np.arange(QK_ROPE)] = 1.0
    w_abs = jnp.concatenate([wk_t, jnp.broadcast_to(jnp.asarray(sel), (N_HEADS, LANES, LANES))], axis=2)
    wv = jnp.transpose(wkv3[:, :, QK_NOPE:], (1, 0, 2))
    w_vh = (wv[:, :, None, :] * jnp.eye(N_HEADS, dtype=f32)[:, None, :, None]).reshape(N_HEADS, KV_LORA, D_ATT)

    row = lambda v: v.reshape(1, -1)
    return {
        "g_pre": row(g_pre), "w_in_t": jnp.swapaxes(w_in, 0, 1).astype(bf16), "conv_w": conv_w,
        "conv_b": row(conv_b), "ln_g": row(ln_g), "ln_b": row(ln_b), "g_qa": row(g_qa), "w_q": w_q.astype(bf16),
        "g_kva": row(g_kva), "w_kv": w_kv.astype(bf16), "e_one": e_one,
        "w_abs": w_abs.astype(bf16), "w_vh": w_vh.astype(bf16),
        "w_out": w_out.astype(bf16), "g_post": row(g_post),
    }


def _rope_tables(start, length):
    inv = ROPE_THETA ** (-np.arange(0, QK_ROPE, 2, dtype=np.float64) / QK_ROPE)
    ang = np.arange(start, start + length, dtype=np.float64)[:, None] * inv[None, :]
    cos, sin = np.cos(ang), np.sin(ang)
    z = lambda n: np.zeros((length, n))
    tail = LANES - QK_NOPE - QK_ROPE
    cs_q = np.concatenate([np.full((length, QK_NOPE), Q_SCALE), cos * Q_SCALE, cos * Q_SCALE, z(tail)], axis=1)
    sn_q = np.concatenate([z(QK_NOPE), sin * Q_SCALE, sin * Q_SCALE, z(tail)], axis=1)
    cs_k = np.concatenate([cos, cos, z(LANES - QK_ROPE)], axis=1)
    sn_k = np.concatenate([-sin, sin, z(LANES - QK_ROPE)], axis=1)
    return tuple(jnp.asarray(t, dtype=f32) for t in (cs_q, sn_q, cs_k, sn_k))


def _layer(x_prompt, x_sample, ckv_past, kr_past, conv_state, w):
    B, S, _ = x_prompt.shape
    Bs, T, _ = x_sample.shape
    past = ckv_past.shape[1]
    pad_hist = lambda hst: jnp.pad(hst, ((0, 0), (HIST_PAD - HIST, 0), (0, 0)))

    qp, kp, vp, ycp, gap, ckvp, krp_t, nhp = _proj_call(
        x_prompt, jnp.zeros((B, HIST_PAD, D_CONV), f32), _rope_tables(0, S), w, 1, PROJ_ROWS, True)
    yap = _attend_prompt_call(qp, kp, vp, gap)
    y_prompt = _out_call(ycp.reshape(B * S, D_CONV), yap.reshape(B * S, D_ATT),
                         x_prompt.reshape(B * S, D_MODEL), w).reshape(B, S, D_MODEL)

    qs, _, _, ycs, gas, ckvs, krs, nhs = _proj_call(
        x_sample, pad_hist(conv_state), _rope_tables(past, T), w, SAMPLE_GROUP, T, False)
    yas = _attend_sample_call(qs, ckvs, krs, ckv_past, jnp.swapaxes(kr_past, 1, 2), w, gas)
    y_sample = _out_call(ycs.reshape(Bs * T, D_CONV), yas.reshape(Bs * T, D_ATT),
                         x_sample.reshape(Bs * T, D_MODEL), w).reshape(Bs, T, D_MODEL)

    trim = lambda nh: nh[:, HIST_PAD - HIST:, :]
    return y_prompt, y_sample, ckvp, jnp.swapaxes(krp_t, 1, 2), trim(nhp), ckvs, krs, trim(nhs)


def kernel(x_prompt, x_sample, cache_ckv, cache_krope, state_conv, g_pre, w_in, conv_w, conv_b, conv_ln_g,
           conv_ln_b, g_qa, w_qb, g_kva, w_kvb, w_out, g_post):
    depth = w_in.shape[0]
    yp, ys = x_prompt, x_sample
    outs = [[] for _ in range(6)]
    for l in range(depth):
        w = _layout_weights(g_pre[l], w_in[l], conv_w[l], conv_b[l], conv_ln_g[l], conv_ln_b[l], g_qa[l],
                            w_qb[l], g_kva[l], w_kvb[l], w_out[l], g_post[l])
        yp, ys, *caches = _layer(yp, ys, cache_ckv[l], cache_krope[l], state_conv[l], w)
        for dst, val in zip(outs, caches):
            dst.append(val)
    return (yp, ys) + tuple(jnp.stack(o) for o in outs)
```

```python
import functools

import jax
import jax.numpy as jnp
import numpy as np
from jax import lax
from jax.experimental import pallas as pl
from jax.experimental.pallas import tpu as pltpu

D_MODEL = 1024
CHUNK = 64
D_CONV = 512
D_ATT = 512
CONV_WIDTH = 31
HIST = CONV_WIDTH - 1
N_HEADS = 8
QK_NOPE = 64
QK_ROPE = 32
V_DIM = 64
Q_LORA = 256
KV_LORA = 128
ROPE_THETA = 10000.0
EPS = 1e-6
ATT_SCALE = (QK_NOPE + QK_ROPE) ** -0.5
Q_SCALE = ATT_SCALE * float(np.log2(np.e))
NEG = -1e30

LANES = 128
SUBLANES = 8
BF16_ROWS = 16
HIST_PAD = 32
HEAD_W = N_HEADS * LANES
ROPE_LO = QK_NOPE
HALF = QK_ROPE // 2
R_A, R_GC, R_QC, R_KVC, R_KR, R_GA, R_END = 0, 1024, 1536, 1792, 1920, 1952, 2464

PROJ_ROWS = 512
ATT_TQ = 256
ATT_HEADS = 2
OUT_ROWS = 2048
OUT_SUB_ROWS = 256
SAMPLE_GROUP = 16
GLU_PARTS = 2
SAMPLE_ATT_GROUP = 4
VMEM_LIMIT = 56 * 1024 * 1024

f32 = jnp.float32
bf16 = jnp.bfloat16
NT = (((1,), (1,)), ((), ()))


def _sigmoid(x):
    return 0.5 * jnp.tanh(0.5 * x) + 0.5


def _silu(x):
    hx = 0.5 * x
    return hx + hx * jnp.tanh(hx)


def _rms(x, g):
    return x * lax.rsqrt(jnp.mean(x * x, axis=-1, keepdims=True) + EPS) * g


def _block_rows(n):
    rows = -(-n // SUBLANES)
    while rows % 8 == 0:
        rows += 1
    return rows


def _conv_geometry(nb, seg):
    unit_rows = seg // GLU_PARTS if nb == 1 else seg
    stride = _block_rows(unit_rows)
    return unit_rows, stride, SUBLANES * stride - unit_rows


def _conv_outputs(slab, row0, stride, i0, n, lanes, convw_ref, convb_ref):
    first = HIST_PAD - HIST
    taps = [jnp.broadcast_to(convw_ref[t:t + 1, lanes], (SUBLANES, LANES)) for t in range(CONV_WIDTH)]
    accs = [jnp.broadcast_to(convb_ref[:, lanes], (SUBLANES, LANES))] * n
    for v in range(i0 + first, i0 + n + first + HIST):
        x = slab[pl.ds(row0 + v, SUBLANES, stride=stride), :]
        for k in range(n):
            t = v - (i0 + k) - first
            if 0 <= t < CONV_WIDTH:
                accs[k] = accs[k] + x * taps[t]
    return accs


def _proj_body(nb, seg, prompt, zero_ref, x_ref, hist_ref, csq_ref, snq_ref, csk_ref, snk_ref, gpre_ref, win_ref,
               convw_ref, convb_ref, lng_ref, lnb_ref, gqa_ref, wq_ref, gkva_ref, wkv_ref, eone_ref, q_ref, *rest):
    k_ref, v_ref = rest[:2] if prompt else (None, None)
    yc_ref, ga_ref, ckv_ref, kr_ref, nh_ref, gbuf, gcbuf, hbuf, ybuf = rest[2 if prompt else 0:]
    rows = nb * seg
    si = pl.program_id(1)

    zero = zero_ref[0]
    hrow0 = pl.multiple_of(zero, BF16_ROWS)
    grow0 = pl.multiple_of(zero, SUBLANES)

    def proj(r0, r1):
        return lax.dot_general(hbuf[pl.ds(hrow0, rows), :], win_ref[r0:r1, :], NT, preferred_element_type=f32)

    n_slabs = D_CONV // LANES
    slab_lanes = [slice(c * LANES, (c + 1) * LANES) for c in range(n_slabs)]
    unit_rows, stride, pad = _conv_geometry(nb, seg)
    units = ([(0, p * unit_rows) for p in range(GLU_PARTS)] if nb == 1
             else [(j, 0) for j in range(nb)])
    spare_g = HIST_PAD + seg + pad
    spare_c = spare_g + SUBLANES

    @pl.when(si == 0)
    def _():
        for c in range(n_slabs):
            gbuf[:, c, 0:HIST_PAD, :] = hist_ref[:, :, slab_lanes[c]]
            for j, base in units if pad else ():
                end = HIST_PAD + base + unit_rows
                gbuf[j, c, end:end + pad, :] = jnp.zeros((pad, LANES), f32)

    part_rows = rows // GLU_PARTS
    part_seqs = nb // GLU_PARTS

    def glu_part(part):
        x = (x_ref[0, part * part_rows:(part + 1) * part_rows, :] if nb == 1
             else x_ref[part * part_seqs:(part + 1) * part_seqs].reshape(part_rows, D_MODEL))
        h = _rms(x, gpre_ref[...]).astype(bf16)
        hbuf[part * part_rows:(part + 1) * part_rows, :] = h
        ab = lax.dot_general(h, win_ref[R_A:R_GC, :], NT, preferred_element_type=f32)
        glu = ab[:, :D_CONV] * _sigmoid(ab[:, D_CONV:])
        for c in range(n_slabs):
            if nb == 1:
                gbuf[0, c, HIST_PAD + part * part_rows:HIST_PAD + (part + 1) * part_rows, :] = glu[:, slab_lanes[c]]
            else:
                gbuf[part * part_seqs:(part + 1) * part_seqs, c, HIST_PAD:HIST_PAD + seg, :] = (
                    glu[:, slab_lanes[c]].reshape(part_seqs, seg, LANES))
        return ab

    glu_part(0)

    def conv_rows(j, base, i0, n):
        accs = []
        for c in range(n_slabs):
            accs.append(_conv_outputs(gbuf.at[j, c], grow0 + base, stride, i0, n, slab_lanes[c], convw_ref, convb_ref))
            if c + 1 < n_slabs:
                gbuf[0, 0, spare_c:spare_c + SUBLANES, :] = accs[-1][-1]
        y = jnp.concatenate([jnp.concatenate([accs[c][k] for c in range(n_slabs)], axis=1) for k in range(n)], axis=0)
        mu = jnp.mean(y, axis=-1, keepdims=True)
        d = y - mu
        var = jnp.mean(d * d, axis=-1, keepdims=True)
        yn = d * lax.rsqrt(var + EPS) * lng_ref[...] + lnb_ref[...]
        ys = _silu(yn)
        for k in range(n):
            for c in range(n_slabs):
                ybuf[j, c, pl.ds(base + i0 + k, SUBLANES, stride=stride), :] = (
                    ys[k * SUBLANES:(k + 1) * SUBLANES, slab_lanes[c]])
        return jnp.concatenate([ys[0:SUBLANES, 0:LANES]] * 2, axis=0).astype(bf16)

    t0 = pl.multiple_of(si * seg, seg)
    rows_of = lambda ref: ref[pl.ds(t0, seg), :]
    carry = {}

    def gate_c():
        gc = proj(R_GC, R_QC)
        gcbuf[...] = _silu(gc).reshape(nb, seg, D_CONV)
        return gc

    def gate_a():
        ga = proj(R_GA, R_END)
        ga_ref[...] = _silu(ga).astype(bf16).reshape(nb, seg, D_ATT)
        return ga

    def q_latent():
        qc = proj(R_QC, R_KVC)
        carry["qn"] = _rms(qc, gqa_ref[...]).astype(bf16)
        return qc

    def q_heads():
        qq = jnp.dot(carry["qn"], wq_ref[...], preferred_element_type=f32)
        csq, snq = rows_of(csq_ref), rows_of(snq_ref)
        for hh in range(N_HEADS):
            lo = hh * LANES
            qa = qq[:, lo:lo + LANES].reshape(nb, seg, LANES)
            qb = qq[:, HEAD_W + lo:HEAD_W + lo + LANES].reshape(nb, seg, LANES)
            q_ref[:, :, lo:lo + LANES] = (qa * csq + qb * snq).astype(bf16)
        return qq

    def k_rope():
        kr = proj(R_KR, R_KR + LANES)
        lane = lax.broadcasted_iota(jnp.int32, (rows, LANES), 1)
        kr_rot = jnp.where(lane < HALF, pltpu.roll(kr, LANES - HALF, axis=1), pltpu.roll(kr, HALF, axis=1))
        krope = kr.reshape(nb, seg, LANES) * rows_of(csk_ref) + kr_rot.reshape(nb, seg, LANES) * rows_of(snk_ref)
        if prompt:
            kr_ref[0] = krope[0].T[0:QK_ROPE, :]
        else:
            kr_ref[...] = krope[:, :, 0:QK_ROPE]
        carry["krope_k"] = pltpu.roll(krope.reshape(rows, LANES), ROPE_LO, axis=1)
        return kr

    def kv_heads():
        ckv = _rms(proj(R_KVC, R_KR), gkva_ref[...])
        ckv_ref[...] = ckv.reshape(nb, seg, KV_LORA)
        if not prompt:
            return ckv
        kk = jnp.dot(ckv.astype(bf16), wkv_ref[...], preferred_element_type=f32)
        is_k = lax.broadcasted_iota(jnp.int32, (rows, LANES), 1) < QK_NOPE
        for hh in range(N_HEADS):
            kv = kk[:, hh * LANES:(hh + 1) * LANES]
            k_ref[:, :, hh * LANES:(hh + 1) * LANES] = (
                jnp.where(is_k, kv, carry["krope_k"]).astype(bf16).reshape(nb, seg, LANES))
            v_ref[:, :, hh * LANES:(hh + 1) * LANES] = (
                jnp.where(is_k, eone_ref[...], kv).astype(bf16).reshape(nb, seg, LANES))
        return kk

    stages = [functools.partial(glu_part, p) for p in range(1, GLU_PARTS)] + [
        gate_c, gate_a, q_latent, q_heads, k_rope, kv_heads]
    per_unit = max(1, -(-len(stages) // len(units)))
    splits = [(stride * k) // per_unit for k in range(per_unit + 1)]
    chunks = [(j, base, splits[k], splits[k + 1] - splits[k]) for j, base in units for k in range(per_unit)]
    assert len(chunks) >= len(stages)
    pending = None
    for n, chunk in enumerate(chunks):
        conv_piece = conv_rows(*chunk)
        if pending is not None:
            gbuf[0, 0, spare_g:spare_g + SUBLANES, :] = pending[0:SUBLANES, 0:LANES]
            pending = None
        if n < len(stages):
            pending = stages[n]()
        hbuf[rows:rows + BF16_ROWS, 0:LANES] = conv_piece
    ys_all = jnp.concatenate([ybuf[:, c, 0:seg, :] for c in range(n_slabs)], axis=2)
    yc_ref[...] = (ys_all * gcbuf[...]).astype(bf16)

    for c in range(n_slabs):
        tail = gbuf[:, c, seg:seg + HIST_PAD, :]
        nh_ref[:, :, slab_lanes[c]] = tail
        gbuf[:, c, 0:HIST_PAD, :] = tail


def _proj_call(x, hist, tables, w, nb, seg, prompt):
    B, T, _ = x.shape
    grid = (B // nb, T // seg)
    pad = _conv_geometry(nb, seg)[2]
    full = lambda a: pl.BlockSpec(a.shape, lambda b, s: (0,) * a.ndim)
    tok = lambda width: pl.BlockSpec((nb, seg, width), lambda b, s: (b, s, 0))
    weights = (w["g_pre"], w["w_in_t"], w["conv_w"], w["conv_b"], w["ln_g"], w["ln_b"],
               w["g_qa"], w["w_q"], w["g_kva"], w["w_kv"], w["e_one"])
    if prompt:
        assert nb == 1
        kr_shape, kr_spec = (B, QK_ROPE, T), pl.BlockSpec((1, QK_ROPE, seg), lambda b, s: (b, 0, s))
    else:
        kr_shape, kr_spec = (B, T, QK_ROPE), tok(QK_ROPE)
    head_out = jax.ShapeDtypeStruct((B, T, HEAD_W), bf16)
    out_shape = (head_out,) + ((head_out, head_out) if prompt else ()) + (
        jax.ShapeDtypeStruct((B, T, D_CONV), bf16),
        jax.ShapeDtypeStruct((B, T, D_ATT), bf16),
        jax.ShapeDtypeStruct((B, T, KV_LORA), f32),
        jax.ShapeDtypeStruct(kr_shape, f32),
        jax.ShapeDtypeStruct((B, HIST_PAD, D_CONV), f32),
    )
    out_specs = (tok(HEAD_W),) * (3 if prompt else 1) + (
        tok(D_CONV), tok(D_ATT), tok(KV_LORA), kr_spec, pl.BlockSpec((nb, HIST_PAD, D_CONV), lambda b, s: (b, 0, 0)))
    return pl.pallas_call(
        functools.partial(_proj_body, nb, seg, prompt),
        grid=grid,
        in_specs=[pl.BlockSpec(memory_space=pltpu.SMEM), tok(D_MODEL),
                  pl.BlockSpec((nb, HIST_PAD, D_CONV), lambda b, s: (b, 0, 0))]
        + [full(t) for t in tables] + [full(a) for a in weights],
        out_specs=out_specs,
        out_shape=out_shape,
        scratch_shapes=[pltpu.VMEM((nb, D_CONV // LANES, HIST_PAD + seg + pad + 2 * SUBLANES, LANES), f32),
                        pltpu.VMEM((nb, seg, D_CONV), f32),
                        pltpu.VMEM((nb * seg + BF16_ROWS, D_MODEL), bf16),
                        pltpu.VMEM((nb, D_CONV // LANES, seg + pad, LANES), f32)],
        compiler_params=pltpu.CompilerParams(dimension_semantics=("arbitrary", "arbitrary"),
                                             vmem_limit_bytes=VMEM_LIMIT),
        name="proj",
    )(jnp.zeros((1,), jnp.int32), x, hist, *tables, *weights)


def _softmax_pv(parts):
    m = None
    for s, _ in parts:
        mi = jnp.max(s, axis=-1, keepdims=True)
        m = mi if m is None else jnp.maximum(m, mi)
    o = None
    for s, v in parts:
        oi = jnp.dot(jnp.exp2(s - m).astype(bf16), v, preferred_element_type=f32)
        o = oi if o is None else o + oi
    return o


def _attend_prompt_body(seq, q_ref, k_ref, v_ref, ga_ref, o_ref):
    tq = ATT_TQ
    row_chunk = lax.broadcasted_iota(jnp.int32, (tq, tq), 0) // CHUNK
    col_chunk = lax.broadcasted_iota(jnp.int32, (tq, tq), 1) // CHUNK
    visible = col_chunk <= row_chunk
    lane = lax.broadcasted_iota(jnp.int32, (tq, LANES), 1)
    def scores(qi, e):
        q0 = qi * tq
        lanes = slice(e * LANES, (e + 1) * LANES)
        qh = q_ref[0, q0:q0 + tq, lanes]
        s_diag = lax.dot_general(qh, k_ref[0, q0:q0 + tq, lanes], NT, preferred_element_type=f32)
        parts = [(jnp.where(visible, s_diag, NEG), v_ref[0, q0:q0 + tq, :])]
        if qi > 0:
            s_off = lax.dot_general(qh, k_ref[0, 0:q0, lanes], NT, preferred_element_type=f32)
            parts.append((s_off, v_ref[0, 0:q0, :]))
        return parts

    items = [(qi, e) for qi in range(seq // tq) for e in range(ATT_HEADS)]
    ahead = 4
    queue = [scores(*it) for it in items[:ahead]]
    heads = []
    for idx, (qi, e) in enumerate(items):
        cur = queue.pop(0)
        if idx + ahead < len(items):
            queue.append(scores(*items[idx + ahead]))
        o = _softmax_pv(cur)[:, e * LANES:(e + 1) * LANES]
        heads.append(o / o[:, 0:1])
        if e % 2 == 1:
            q0 = qi * tq
            out_lanes = slice((e // 2) * LANES, (e // 2 + 1) * LANES)
            y = jnp.where(lane < V_DIM, pltpu.roll(heads[0], V_DIM, axis=1), heads[1])
            o_ref[0, q0:q0 + tq, out_lanes] = (y * ga_ref[0, q0:q0 + tq, out_lanes].astype(f32)).astype(bf16)
            heads = []


def _attend_prompt_call(q, k, v, ga):
    B, S, _ = q.shape
    group = pl.BlockSpec((1, S, ATT_HEADS * LANES), lambda b, p: (b, 0, p))
    half = pl.BlockSpec((1, S, ATT_HEADS * V_DIM), lambda b, p: (b, 0, p))
    return pl.pallas_call(
        functools.partial(_attend_prompt_body, S),
        grid=(B, N_HEADS // ATT_HEADS),
        in_specs=[group, group, group, half],
        out_specs=half,
        out_shape=jax.ShapeDtypeStruct((B, S, D_ATT), bf16),
        compiler_params=pltpu.CompilerParams(dimension_semantics=("arbitrary", "arbitrary"),
                                             vmem_limit_bytes=VMEM_LIMIT),
        name="attend_prompt",
    )(q, k, v, ga)


def _attend_sample_body(q_ref, ckvn_ref, krn_ref, ckvp_ref, krpt_ref, wabs_ref, wv_ref, ga_ref, o_ref):
    g_seqs, t, _ = q_ref.shape
    q = q_ref[...].reshape(g_seqs * t, HEAD_W)
    q_cat = [jnp.dot(q[:, hh * LANES:(hh + 1) * LANES], wabs_ref[hh], preferred_element_type=f32).astype(bf16)
             for hh in range(N_HEADS)]

    def scores(g):
        qg = jnp.concatenate([qc[g * t:(g + 1) * t] for qc in q_cat], axis=0)
        q_lat = qg[:, :KV_LORA]
        q_rope = qg[:, KV_LORA:KV_LORA + QK_ROPE]
        ckv_p = ckvp_ref[g].astype(bf16)
        ckv_n = ckvn_ref[g].astype(bf16)
        s_p = (lax.dot_general(q_lat, ckv_p, NT, preferred_element_type=f32)
               + jnp.dot(q_rope, krpt_ref[g].astype(bf16), preferred_element_type=f32))
        s_n = (lax.dot_general(q_lat, ckv_n, NT, preferred_element_type=f32)
               + lax.dot_general(q_rope, krn_ref[g].astype(bf16), NT, preferred_element_type=f32))
        return s_p, s_n, ckv_p, ckv_n

    def latent_out(s_p, s_n, ckv_p, ckv_n):
        m = jnp.maximum(jnp.max(s_p, axis=-1, keepdims=True), jnp.max(s_n, axis=-1, keepdims=True))
        p_p = jnp.exp2(s_p - m).astype(bf16)
        p_n = jnp.exp2(s_n - m).astype(bf16)
        l = jnp.sum(p_p.astype(f32), axis=-1, keepdims=True) + jnp.sum(p_n.astype(f32), axis=-1, keepdims=True)
        o_lat = (jnp.dot(p_p, ckv_p, preferred_element_type=f32) + jnp.dot(p_n, ckv_n, preferred_element_type=f32)) / l
        return o_lat.astype(bf16)

    nxt = scores(0)
    o_lat = []
    for g in range(g_seqs):
        cur = nxt
        if g + 1 < g_seqs:
            nxt = scores(g + 1)
        o_lat.append(latent_out(*cur))
    y = None
    for hh in range(N_HEADS):
        o_h = jnp.concatenate([o[hh * t:(hh + 1) * t] for o in o_lat], axis=0)
        yh = jnp.dot(o_h, wv_ref[hh], preferred_element_type=f32)
        y = yh if y is None else y + yh
    o_ref[...] = (y.reshape(g_seqs, t, D_ATT) * ga_ref[...].astype(f32)).astype(bf16)


def _attend_sample_call(q, ckv_new, kr_new, ckv_past, kr_past_t, w, ga):
    B, T, _ = q.shape
    P = ckv_past.shape[1]
    per_b = lambda shape: pl.BlockSpec((SAMPLE_ATT_GROUP,) + shape, lambda b: (b, 0, 0))
    full = lambda a: pl.BlockSpec(a.shape, lambda b: (0,) * a.ndim)
    return pl.pallas_call(
        _attend_sample_body,
        grid=(B // SAMPLE_ATT_GROUP,),
        in_specs=[per_b((T, HEAD_W)), per_b((T, KV_LORA)), per_b((T, QK_ROPE)),
                  per_b((P, KV_LORA)), per_b((QK_ROPE, P)), full(w["w_abs"]), full(w["w_vh"]),
                  per_b((T, D_ATT))],
        out_specs=per_b((T, D_ATT)),
        out_shape=jax.ShapeDtypeStruct((B, T, D_ATT), bf16),
        compiler_params=pltpu.CompilerParams(dimension_semantics=("arbitrary",),
                                             vmem_limit_bytes=VMEM_LIMIT),
        name="attend_sample",
    )(q, ckv_new, kr_new, ckv_past, kr_past_t, w["w_abs"], w["w_vh"], ga)


def _out_body(yc_ref, ya_ref, x_ref, wout_ref, gpost_ref, y_ref):
    rows = y_ref.shape[0]
    sub = min(OUT_SUB_ROWS, rows)

    def mix(i):
        r = slice(i * sub, (i + 1) * sub)
        return (jnp.dot(yc_ref[r, :], wout_ref[0:D_CONV, :], preferred_element_type=f32)
                + jnp.dot(ya_ref[r, :], wout_ref[D_CONV:, :], preferred_element_type=f32))

    nxt = mix(0)
    for i in range(rows // sub):
        cur = nxt
        if i + 1 < rows // sub:
            nxt = mix(i + 1)
        r = slice(i * sub, (i + 1) * sub)
        y_ref[r, :] = x_ref[r, :] + _rms(cur, gpost_ref[...])


def _out_call(yc, ya, x, w):
    n = x.shape[0]
    rows = min(OUT_ROWS, n // 2)
    tok = lambda width: pl.BlockSpec((rows, width), lambda i: (i, 0))
    full = lambda a: pl.BlockSpec(a.shape, lambda i: (0,) * a.ndim)
    return pl.pallas_call(
        _out_body,
        grid=(n // rows,),
        in_specs=[tok(D_CONV), tok(D_ATT), tok(D_MODEL), full(w["w_out"]), full(w["g_post"])],
        out_specs=tok(D_MODEL),
        out_shape=jax.ShapeDtypeStruct((n, D_MODEL), f32),
        compiler_params=pltpu.CompilerParams(dimension_semantics=("arbitrary",),
                                             vmem_limit_bytes=VMEM_LIMIT),
        name="out",
    )(yc, ya, x, w["w_out"], w["g_post"])


def _rot_cols(rope):
    return jnp.concatenate([-rope[..., HALF:], rope[..., :HALF]], axis=-1)


def _layout_weights(g_pre, w_in, conv_w, conv_b, ln_g, ln_b, g_qa, w_qb, g_kva, w_kvb, w_out, g_post):
    dq = QK_NOPE + QK_ROPE
    pad_q = ((0, 0), (0, 0), (0, LANES - dq))
    wq3 = w_qb.reshape(Q_LORA, N_HEADS, dq)
    w_q_plain = jnp.pad(wq3, pad_q).reshape(Q_LORA, HEAD_W)
    w_q_rot = jnp.pad(_rot_cols(wq3[:, :, QK_NOPE:]), ((0, 0), (0, 0), (ROPE_LO, LANES - dq))).reshape(Q_LORA, HEAD_W)
    w_q = jnp.concatenate([w_q_plain, w_q_rot], axis=1)

    w_kv = w_kvb
    e_one = (jnp.arange(LANES) == 0).astype(f32)[None]

    wkv3 = w_kvb.reshape(KV_LORA, N_HEADS, QK_NOPE + V_DIM)
    wk_t = jnp.transpose(wkv3[:, :, :QK_NOPE], (1, 2, 0))
    wk_t = jnp.pad(wk_t, ((0, 0), (0, LANES - QK_NOPE), (0, 0)))
    sel = np.zeros((LANES, LANES), np.float32)
    sel[ROPE_LO + np.arange(QK_ROPE), np.arange(QK_ROPE)] = 1.0
    w_abs = jnp.concatenate([wk_t, jnp.broadcast_to(jnp.asarray(sel), (N_HEADS, LANES, LANES))], axis=2)
    wv = jnp.transpose(wkv3[:, :, QK_NOPE:], (1, 0, 2))
    w_vh = (wv[:, :, None, :] * jnp.eye(N_HEADS, dtype=f32)[:, None, :, None]).reshape(N_HEADS, KV_LORA, D_ATT)

    row = lambda v: v.reshape(1, -1)
    return {
        "g_pre": row(g_pre), "w_in_t": jnp.swapaxes(w_in, 0, 1).astype(bf16), "conv_w": conv_w,
        "conv_b": row(conv_b), "ln_g": row(ln_g), "ln_b": row(ln_b), "g_qa": row(g_qa), "w_q": w_q.astype(bf16),
        "g_kva": row(g_kva), "w_kv": w_kv.astype(bf16), "e_one": e_one,
        "w_abs": w_abs.astype(bf16), "w_vh": w_vh.astype(bf16),
        "w_out": w_out.astype(bf16), "g_post": row(g_post),
    }


def _rope_tables(start, length):
    inv = ROPE_THETA ** (-np.arange(0, QK_ROPE, 2, dtype=np.float64) / QK_ROPE)
    ang = np.arange(start, start + length, dtype=np.float64)[:, None] * inv[None, :]
    cos, sin = np.cos(ang), np.sin(ang)
    z = lambda n: np.zeros((length, n))
    tail = LANES - QK_NOPE - QK_ROPE
    cs_q = np.concatenate([np.full((length, QK_NOPE), Q_SCALE), cos * Q_SCALE, cos * Q_SCALE, z(tail)], axis=1)
    sn_q = np.concatenate([z(QK_NOPE), sin * Q_SCALE, sin * Q_SCALE, z(tail)], axis=1)
    cs_k = np.concatenate([cos, cos, z(LANES - QK_ROPE)], axis=1)
    sn_k = np.concatenate([-sin, sin, z(LANES - QK_ROPE)], axis=1)
    return tuple(jnp.asarray(t, dtype=f32) for t in (cs_q, sn_q, cs_k, sn_k))


def _layer(x_prompt, x_sample, ckv_past, kr_past, conv_state, w):
    B, S, _ = x_prompt.shape
    Bs, T, _ = x_sample.shape
    past = ckv_past.shape[1]
    pad_hist = lambda hst: jnp.pad(hst, ((0, 0), (HIST_PAD - HIST, 0), (0, 0)))

    qp, kp, vp, ycp, gap, ckvp, krp_t, nhp = _proj_call(
        x_prompt, jnp.zeros((B, HIST_PAD, D_CONV), f32), _rope_tables(0, S), w, 1, PROJ_ROWS, True)
    yap = _attend_prompt_call(qp, kp, vp, gap)
    y_prompt = _out_call(ycp.reshape(B * S, D_CONV), yap.reshape(B * S, D_ATT),
                         x_prompt.reshape(B * S, D_MODEL), w).reshape(B, S, D_MODEL)

    qs, ycs, gas, ckvs, krs, nhs = _proj_call(
        x_sample, pad_hist(conv_state), _rope_tables(past, T), w, SAMPLE_GROUP, T, False)
    yas = _attend_sample_call(qs, ckvs, krs, ckv_past, jnp.swapaxes(kr_past, 1, 2), w, gas)
    y_sample = _out_call(ycs.reshape(Bs * T, D_CONV), yas.reshape(Bs * T, D_ATT),
                         x_sample.reshape(Bs * T, D_MODEL), w).reshape(Bs, T, D_MODEL)

    trim = lambda nh: nh[:, HIST_PAD - HIST:, :]
    return y_prompt, y_sample, ckvp, jnp.swapaxes(krp_t, 1, 2), trim(nhp), ckvs, krs, trim(nhs)


def kernel(x_prompt, x_sample, cache_ckv, cache_krope, state_conv, g_pre, w_in, conv_w, conv_b, conv_ln_g,
           conv_ln_b, g_qa, w_qb, g_kva, w_kvb, w_out, g_post):
    depth = w_in.shape[0]
    yp, ys = x_prompt, x_sample
    outs = [[] for _ in range(6)]
    for l in range(depth):
        w = _layout_weights(g_pre[l], w_in[l], conv_w[l], conv_b[l], conv_ln_g[l], conv_ln_b[l], g_qa[l],
                            w_qb[l], g_kva[l], w_kvb[l], w_out[l], g_post[l])
        yp, ys, *caches = _layer(yp, ys, cache_ckv[l], cache_krope[l], state_conv[l], w)
        for dst, val in zip(outs, caches):
            dst.append(val)
    return (yp, ys) + tuple(jnp.stack(o) for o in outs)
```

```python
import functools

import jax
import jax.numpy as jnp
import numpy as np
from jax import lax
from jax.experimental import pallas as pl
from jax.experimental.pallas import tpu as pltpu

D_MODEL = 1024
CHUNK = 64
D_CONV = 512
D_ATT = 512
CONV_WIDTH = 31
HIST = CONV_WIDTH - 1
N_HEADS = 8
QK_NOPE = 64
QK_ROPE = 32
V_DIM = 64
Q_LORA = 256
KV_LORA = 128
ROPE_THETA = 10000.0
EPS = 1e-6
ATT_SCALE = (QK_NOPE + QK_ROPE) ** -0.5
Q_SCALE = ATT_SCALE * float(np.log2(np.e))
NEG = -1e30

LANES = 128
SUBLANES = 8
BF16_ROWS = 16
HIST_PAD = 32
HEAD_W = N_HEADS * LANES
ROPE_LO = QK_NOPE
HALF = QK_ROPE // 2
R_A, R_GC, R_QC, R_KVC, R_KR, R_GA, R_END = 0, 1024, 1536, 1792, 1920, 1952, 2464

PROJ_ROWS = 512
ATT_TQ = 256
ATT_HEADS = 2
ATT_AHEAD = 4
OUT_ROWS = 2048
OUT_SUB_ROWS = 256
SAMPLE_GROUP = 16
GLU_PARTS = 2
SAMPLE_ATT_GROUP = 4
VMEM_LIMIT = 56 * 1024 * 1024

f32 = jnp.float32
bf16 = jnp.bfloat16
NT = (((1,), (1,)), ((), ()))


def _sigmoid(x):
    return 0.5 * jnp.tanh(0.5 * x) + 0.5


def _silu(x):
    hx = 0.5 * x
    return hx + hx * jnp.tanh(hx)


def _rms(x, g):
    return x * lax.rsqrt(jnp.mean(x * x, axis=-1, keepdims=True) + EPS) * g


def _block_rows(n):
    rows = -(-n // SUBLANES)
    while rows % 8 == 0:
        rows += 1
    return rows


def _conv_geometry(nb, seg):
    unit_rows = seg // GLU_PARTS if nb == 1 else seg
    stride = _block_rows(unit_rows)
    return unit_rows, stride, SUBLANES * stride - unit_rows


def _conv_outputs(slab, row0, stride, i0, n, lanes, convw_ref, convb_ref):
    first = HIST_PAD - HIST
    taps = [jnp.broadcast_to(convw_ref[t:t + 1, lanes], (SUBLANES, LANES)) for t in range(CONV_WIDTH)]
    accs = [jnp.broadcast_to(convb_ref[:, lanes], (SUBLANES, LANES))] * n
    for v in range(i0 + first, i0 + n + first + HIST):
        x = slab[pl.ds(row0 + v, SUBLANES, stride=stride), :]
        for k in range(n):
            t = v - (i0 + k) - first
            if 0 <= t < CONV_WIDTH:
                accs[k] = accs[k] + x * taps[t]
    return accs


def _proj_body(nb, seg, prompt, zero_ref, x_ref, hist_ref, csq_ref, snq_ref, csk_ref, snk_ref, gpre_ref, win_ref,
               convw_ref, convb_ref, lng_ref, lnb_ref, gqa_ref, wq_ref, gkva_ref, wkv_ref, eone_ref, q_ref, *rest):
    k_ref, v_ref = rest[:2] if prompt else (None, None)
    yc_ref, ga_ref, ckv_ref, kr_ref, nh_ref, gbuf, gcbuf, hbuf, ybuf = rest[2 if prompt else 0:]
    rows = nb * seg
    si = pl.program_id(1)

    zero = zero_ref[0]
    hrow0 = pl.multiple_of(zero, BF16_ROWS)
    grow0 = pl.multiple_of(zero, SUBLANES)

    def proj(r0, r1):
        return lax.dot_general(hbuf[pl.ds(hrow0, rows), :], win_ref[r0:r1, :], NT, preferred_element_type=f32)

    n_slabs = D_CONV // LANES
    slab_lanes = [slice(c * LANES, (c + 1) * LANES) for c in range(n_slabs)]
    unit_rows, stride, pad = _conv_geometry(nb, seg)
    units = ([(0, p * unit_rows) for p in range(GLU_PARTS)] if nb == 1
             else [(j, 0) for j in range(nb)])
    spare_g = HIST_PAD + seg + pad
    spare_c = spare_g + SUBLANES

    @pl.when(si == 0)
    def _():
        for c in range(n_slabs):
            gbuf[:, c, 0:HIST_PAD, :] = hist_ref[:, :, slab_lanes[c]]
            for j, base in units if pad else ():
                end = HIST_PAD + base + unit_rows
                gbuf[j, c, end:end + pad, :] = jnp.zeros((pad, LANES), f32)

    part_rows = rows // GLU_PARTS
    part_seqs = nb // GLU_PARTS

    def glu_part(part):
        x = (x_ref[0, part * part_rows:(part + 1) * part_rows, :] if nb == 1
             else x_ref[part * part_seqs:(part + 1) * part_seqs].reshape(part_rows, D_MODEL))
        h = _rms(x, gpre_ref[...]).astype(bf16)
        hbuf[part * part_rows:(part + 1) * part_rows, :] = h
        ab = lax.dot_general(h, win_ref[R_A:R_GC, :], NT, preferred_element_type=f32)
        glu = ab[:, :D_CONV] * _sigmoid(ab[:, D_CONV:])
        for c in range(n_slabs):
            if nb == 1:
                gbuf[0, c, HIST_PAD + part * part_rows:HIST_PAD + (part + 1) * part_rows, :] = glu[:, slab_lanes[c]]
            else:
                gbuf[part * part_seqs:(part + 1) * part_seqs, c, HIST_PAD:HIST_PAD + seg, :] = (
                    glu[:, slab_lanes[c]].reshape(part_seqs, seg, LANES))
        return ab

    glu_part(0)

    def conv_rows(j, base, i0, n):
        accs = []
        for c in range(n_slabs):
            accs.append(_conv_outputs(gbuf.at[j, c], grow0 + base, stride, i0, n, slab_lanes[c], convw_ref, convb_ref))
            if c + 1 < n_slabs:
                gbuf[0, 0, spare_c:spare_c + SUBLANES, :] = accs[-1][-1]
        y = jnp.concatenate([jnp.concatenate([accs[c][k] for c in range(n_slabs)], axis=1) for k in range(n)], axis=0)
        mu = jnp.mean(y, axis=-1, keepdims=True)
        d = y - mu
        var = jnp.mean(d * d, axis=-1, keepdims=True)
        yn = d * lax.rsqrt(var + EPS) * lng_ref[...] + lnb_ref[...]
        ys = _silu(yn)
        for k in range(n):
            for c in range(n_slabs):
                ybuf[j, c, pl.ds(base + i0 + k, SUBLANES, stride=stride), :] = (
                    ys[k * SUBLANES:(k + 1) * SUBLANES, slab_lanes[c]])
        return jnp.concatenate([ys[0:SUBLANES, 0:LANES]] * 2, axis=0).astype(bf16)

    t0 = pl.multiple_of(si * seg, seg)
    rows_of = lambda ref: ref[pl.ds(t0, seg), :]
    carry = {}

    def gate_c():
        gc = proj(R_GC, R_QC)
        gcbuf[...] = _silu(gc).reshape(nb, seg, D_CONV)
        return gc

    def gate_a():
        ga = proj(R_GA, R_END)
        ga_ref[...] = _silu(ga).astype(bf16).reshape(nb, seg, D_ATT)
        return ga

    def q_latent():
        qc = proj(R_QC, R_KVC)
        carry["qn"] = _rms(qc, gqa_ref[...]).astype(bf16)
        return qc

    def q_heads():
        qq = jnp.dot(carry["qn"], wq_ref[...], preferred_element_type=f32)
        csq, snq = rows_of(csq_ref), rows_of(snq_ref)
        for hh in range(N_HEADS):
            lo = hh * LANES
            qa = qq[:, lo:lo + LANES].reshape(nb, seg, LANES)
            qb = qq[:, HEAD_W + lo:HEAD_W + lo + LANES].reshape(nb, seg, LANES)
            q_ref[:, :, lo:lo + LANES] = (qa * csq + qb * snq).astype(bf16)
        return qq

    def k_rope():
        kv_kr = proj(R_KVC, R_KR + LANES)
        carry["kvc"], kr = kv_kr[:, :KV_LORA], kv_kr[:, KV_LORA:]
        lane = lax.broadcasted_iota(jnp.int32, (rows, LANES), 1)
        kr_rot = jnp.where(lane < HALF, pltpu.roll(kr, LANES - HALF, axis=1), pltpu.roll(kr, HALF, axis=1))
        krope = kr.reshape(nb, seg, LANES) * rows_of(csk_ref) + kr_rot.reshape(nb, seg, LANES) * rows_of(snk_ref)
        if prompt:
            kr_ref[0] = krope[0].T[0:QK_ROPE, :]
        else:
            kr_ref[...] = krope[:, :, 0:QK_ROPE]
        carry["krope_k"] = pltpu.roll(krope.reshape(rows, LANES), ROPE_LO, axis=1)
        return kr

    def kv_heads():
        ckv = _rms(carry["kvc"], gkva_ref[...])
        ckv_ref[...] = ckv.reshape(nb, seg, KV_LORA)
        if not prompt:
            return ckv
        kk = jnp.dot(ckv.astype(bf16), wkv_ref[...], preferred_element_type=f32)
        is_k = lax.broadcasted_iota(jnp.int32, (rows, LANES), 1) < QK_NOPE
        for hh in range(N_HEADS):
            kv = kk[:, hh * LANES:(hh + 1) * LANES]
            k_ref[:, :, hh * LANES:(hh + 1) * LANES] = (
                jnp.where(is_k, kv, carry["krope_k"]).astype(bf16).reshape(nb, seg, LANES))
            v_ref[:, :, hh * LANES:(hh + 1) * LANES] = (
                jnp.where(is_k, eone_ref[...], kv).astype(bf16).reshape(nb, seg, LANES))
        return kk

    stages = [functools.partial(glu_part, p) for p in range(1, GLU_PARTS)] + [
        q_latent, k_rope, q_heads, kv_heads, gate_a, gate_c]
    per_unit = max(1, -(-len(stages) // len(units)))
    splits = [(stride * k) // per_unit for k in range(per_unit + 1)]
    chunks = [(j, base, splits[k], splits[k + 1] - splits[k]) for j, base in units for k in range(per_unit)]
    assert len(chunks) >= len(stages)
    pending = None
    for n, chunk in enumerate(chunks):
        conv_piece = conv_rows(*chunk)
        if pending is not None:
            gbuf[0, 0, spare_g:spare_g + SUBLANES, :] = pending[0:SUBLANES, 0:LANES]
            pending = None
        if n < len(stages):
            pending = stages[n]()
        hbuf[rows:rows + BF16_ROWS, 0:LANES] = conv_piece
    ys_all = jnp.concatenate([ybuf[:, c, 0:seg, :] for c in range(n_slabs)], axis=2)
    yc_ref[...] = (ys_all * gcbuf[...]).astype(bf16)

    for c in range(n_slabs):
        tail = gbuf[:, c, seg:seg + HIST_PAD, :]
        nh_ref[:, :, slab_lanes[c]] = tail
        gbuf[:, c, 0:HIST_PAD, :] = tail


def _proj_call(x, hist, tables, w, nb, seg, prompt):
    B, T, _ = x.shape
    grid = (B // nb, T // seg)
    pad = _conv_geometry(nb, seg)[2]
    full = lambda a: pl.BlockSpec(a.shape, lambda b, s: (0,) * a.ndim)
    tok = lambda width: pl.BlockSpec((nb, seg, width), lambda b, s: (b, s, 0))
    weights = (w["g_pre"], w["w_in_t"], w["conv_w"], w["conv_b"], w["ln_g"], w["ln_b"],
               w["g_qa"], w["w_q"], w["g_kva"], w["w_kv"], w["e_one"])
    if prompt:
        assert nb == 1
        kr_shape, kr_spec = (B, QK_ROPE, T), pl.BlockSpec((1, QK_ROPE, seg), lambda b, s: (b, 0, s))
    else:
        kr_shape, kr_spec = (B, T, QK_ROPE), tok(QK_ROPE)
    head_out = jax.ShapeDtypeStruct((B, T, HEAD_W), bf16)
    out_shape = (head_out,) + ((head_out, head_out) if prompt else ()) + (
        jax.ShapeDtypeStruct((B, T, D_CONV), bf16),
        jax.ShapeDtypeStruct((B, T, D_ATT), bf16),
        jax.ShapeDtypeStruct((B, T, KV_LORA), f32),
        jax.ShapeDtypeStruct(kr_shape, f32),
        jax.ShapeDtypeStruct((B, HIST_PAD, D_CONV), f32),
    )
    out_specs = (tok(HEAD_W),) * (3 if prompt else 1) + (
        tok(D_CONV), tok(D_ATT), tok(KV_LORA), kr_spec, pl.BlockSpec((nb, HIST_PAD, D_CONV), lambda b, s: (b, 0, 0)))
    return pl.pallas_call(
        functools.partial(_proj_body, nb, seg, prompt),
        grid=grid,
        in_specs=[pl.BlockSpec(memory_space=pltpu.SMEM), tok(D_MODEL),
                  pl.BlockSpec((nb, HIST_PAD, D_CONV), lambda b, s: (b, 0, 0))]
        + [full(t) for t in tables] + [full(a) for a in weights],
        out_specs=out_specs,
        out_shape=out_shape,
        scratch_shapes=[pltpu.VMEM((nb, D_CONV // LANES, HIST_PAD + seg + pad + 2 * SUBLANES, LANES), f32),
                        pltpu.VMEM((nb, seg, D_CONV), f32),
                        pltpu.VMEM((nb * seg + BF16_ROWS, D_MODEL), bf16),
                        pltpu.VMEM((nb, D_CONV // LANES, seg + pad, LANES), f32)],
        compiler_params=pltpu.CompilerParams(dimension_semantics=("arbitrary", "arbitrary"),
                                             vmem_limit_bytes=VMEM_LIMIT),
        name="proj",
    )(jnp.zeros((1,), jnp.int32), x, hist, *tables, *weights)


def _softmax_pv(parts):
    m = None
    for s, _ in parts:
        mi = jnp.max(s, axis=-1, keepdims=True)
        m = mi if m is None else jnp.maximum(m, mi)
    o = None
    for s, v in parts:
        oi = jnp.dot(jnp.exp2(s - m).astype(bf16), v, preferred_element_type=f32)
        o = oi if o is None else o + oi
    return o


def _attend_prompt_body(seq, q_ref, k_ref, v_ref, ga_ref, o_ref):
    tq = ATT_TQ
    row_chunk = lax.broadcasted_iota(jnp.int32, (tq, tq), 0) // CHUNK
    col_chunk = lax.broadcasted_iota(jnp.int32, (tq, tq), 1) // CHUNK
    visible = col_chunk <= row_chunk
    lane = lax.broadcasted_iota(jnp.int32, (tq, LANES), 1)

    def scores(qi, e):
        q0 = qi * tq
        lanes = slice(e * LANES, (e + 1) * LANES)
        pair = slice((e // 2) * 2 * LANES, (e // 2 + 1) * 2 * LANES)
        qh = q_ref[0, q0:q0 + tq, lanes]
        s_diag = lax.dot_general(qh, k_ref[0, q0:q0 + tq, lanes], NT, preferred_element_type=f32)
        parts = [(jnp.where(visible, s_diag, NEG), v_ref[0, q0:q0 + tq, pair])]
        if qi > 0:
            s_off = lax.dot_general(qh, k_ref[0, 0:q0, lanes], NT, preferred_element_type=f32)
            parts.append((s_off, v_ref[0, 0:q0, pair]))
        return parts

    items = [(qi, e) for qi in range(seq // tq) for e in range(ATT_HEADS)]
    ahead = ATT_AHEAD
    queue = [scores(*it) for it in items[:ahead]]
    heads = []
    for idx, (qi, e) in enumerate(items):
        cur = queue.pop(0)
        if idx + ahead < len(items):
            queue.append(scores(*items[idx + ahead]))
        o = _softmax_pv(cur)[:, (e % 2) * LANES:(e % 2 + 1) * LANES]
        heads.append(o / o[:, 0:1])
        if e % 2 == 1:
            q0 = qi * tq
            out_lanes = slice((e // 2) * LANES, (e // 2 + 1) * LANES)
            y = jnp.where(lane < V_DIM, pltpu.roll(heads[0], V_DIM, axis=1), heads[1])
            o_ref[0, q0:q0 + tq, out_lanes] = (y * ga_ref[0, q0:q0 + tq, out_lanes].astype(f32)).astype(bf16)
            heads = []


def _attend_prompt_call(q, k, v, ga):
    B, S, _ = q.shape
    group = pl.BlockSpec((1, S, ATT_HEADS * LANES), lambda b, p: (b, 0, p))
    half = pl.BlockSpec((1, S, ATT_HEADS * V_DIM), lambda b, p: (b, 0, p))
    return pl.pallas_call(
        functools.partial(_attend_prompt_body, S),
        grid=(B, N_HEADS // ATT_HEADS),
        in_specs=[group, group, group, half],
        out_specs=half,
        out_shape=jax.ShapeDtypeStruct((B, S, D_ATT), bf16),
        compiler_params=pltpu.CompilerParams(dimension_semantics=("arbitrary", "arbitrary"),
                                             vmem_limit_bytes=VMEM_LIMIT),
        name="attend_prompt",
    )(q, k, v, ga)


def _attend_sample_body(q_ref, ckvn_ref, krn_ref, ckvp_ref, krpt_ref, wabs_ref, wv_ref, ga_ref, o_ref):
    g_seqs, t, _ = q_ref.shape
    q = q_ref[...].reshape(g_seqs * t, HEAD_W)
    q_cat = [jnp.dot(q[:, hh * LANES:(hh + 1) * LANES], wabs_ref[hh], preferred_element_type=f32).astype(bf16)
             for hh in range(N_HEADS)]

    def scores(g):
        qg = jnp.concatenate([qc[g * t:(g + 1) * t] for qc in q_cat], axis=0)
        q_lat = qg[:, :KV_LORA]
        q_rope = qg[:, KV_LORA:KV_LORA + QK_ROPE]
        ckv_p = ckvp_ref[g].astype(bf16)
        ckv_n = ckvn_ref[g].astype(bf16)
        s_p = (lax.dot_general(q_lat, ckv_p, NT, preferred_element_type=f32)
               + jnp.dot(q_rope, krpt_ref[g].astype(bf16), preferred_element_type=f32))
        s_n = (lax.dot_general(q_lat, ckv_n, NT, preferred_element_type=f32)
               + lax.dot_general(q_rope, krn_ref[g].astype(bf16), NT, preferred_element_type=f32))
        return s_p, s_n, ckv_p, ckv_n

    def latent_out(s_p, s_n, ckv_p, ckv_n):
        m = jnp.maximum(jnp.max(s_p, axis=-1, keepdims=True), jnp.max(s_n, axis=-1, keepdims=True))
        p_p = jnp.exp2(s_p - m).astype(bf16)
        p_n = jnp.exp2(s_n - m).astype(bf16)
        l = jnp.sum(p_p.astype(f32), axis=-1, keepdims=True) + jnp.sum(p_n.astype(f32), axis=-1, keepdims=True)
        o_lat = (jnp.dot(p_p, ckv_p, preferred_element_type=f32) + jnp.dot(p_n, ckv_n, preferred_element_type=f32)) / l
        return o_lat.astype(bf16)

    nxt = scores(0)
    o_lat = []
    for g in range(g_seqs):
        cur = nxt
        if g + 1 < g_seqs:
            nxt = scores(g + 1)
        o_lat.append(latent_out(*cur))
    y = None
    for hh in range(N_HEADS):
        o_h = jnp.concatenate([o[hh * t:(hh + 1) * t] for o in o_lat], axis=0)
        yh = jnp.dot(o_h, wv_ref[hh], preferred_element_type=f32)
        y = yh if y is None else y + yh
    o_ref[...] = (y.reshape(g_seqs, t, D_ATT) * ga_ref[...].astype(f32)).astype(bf16)


def _attend_sample_call(q, ckv_new, kr_new, ckv_past, kr_past_t, w, ga):
    B, T, _ = q.shape
    P = ckv_past.shape[1]
    per_b = lambda shape: pl.BlockSpec((SAMPLE_ATT_GROUP,) + shape, lambda b: (b, 0, 0))
    full = lambda a: pl.BlockSpec(a.shape, lambda b: (0,) * a.ndim)
    return pl.pallas_call(
        _attend_sample_body,
        grid=(B // SAMPLE_ATT_GROUP,),
        in_specs=[per_b((T, HEAD_W)), per_b((T, KV_LORA)), per_b((T, QK_ROPE)),
                  per_b((P, KV_LORA)), per_b((QK_ROPE, P)), full(w["w_abs"]), full(w["w_vh"]),
                  per_b((T, D_ATT))],
        out_specs=per_b((T, D_ATT)),
        out_shape=jax.ShapeDtypeStruct((B, T, D_ATT), bf16),
        compiler_params=pltpu.CompilerParams(dimension_semantics=("arbitrary",),
                                             vmem_limit_bytes=VMEM_LIMIT),
        name="attend_sample",
    )(q, ckv_new, kr_new, ckv_past, kr_past_t, w["w_abs"], w["w_vh"], ga)


def _out_body(yc_ref, ya_ref, x_ref, wout_ref, gpost_ref, y_ref):
    rows = y_ref.shape[0]
    sub = min(OUT_SUB_ROWS, rows)

    def mix(i):
        r = slice(i * sub, (i + 1) * sub)
        return (jnp.dot(yc_ref[r, :], wout_ref[0:D_CONV, :], preferred_element_type=f32)
                + jnp.dot(ya_ref[r, :], wout_ref[D_CONV:, :], preferred_element_type=f32))

    nxt = mix(0)
    for i in range(rows // sub):
        cur = nxt
        if i + 1 < rows // sub:
            nxt = mix(i + 1)
        r = slice(i * sub, (i + 1) * sub)
        y_ref[r, :] = x_ref[r, :] + _rms(cur, gpost_ref[...])


def _out_call(yc, ya, x, w):
    n = x.shape[0]
    rows = min(OUT_ROWS, n // 2)
    tok = lambda width: pl.BlockSpec((rows, width), lambda i: (i, 0))
    full = lambda a: pl.BlockSpec(a.shape, lambda i: (0,) * a.ndim)
    return pl.pallas_call(
        _out_body,
        grid=(n // rows,),
        in_specs=[tok(D_CONV), tok(D_ATT), tok(D_MODEL), full(w["w_out"]), full(w["g_post"])],
        out_specs=tok(D_MODEL),
        out_shape=jax.ShapeDtypeStruct((n, D_MODEL), f32),
        compiler_params=pltpu.CompilerParams(dimension_semantics=("arbitrary",),
                                             vmem_limit_bytes=VMEM_LIMIT),
        name="out",
    )(yc, ya, x, w["w_out"], w["g_post"])


def _rot_cols(rope):
    return jnp.concatenate([-rope[..., HALF:], rope[..., :HALF]], axis=-1)


def _layout_weights(g_pre, w_in, conv_w, conv_b, ln_g, ln_b, g_qa, w_qb, g_kva, w_kvb, w_out, g_post):
    dq = QK_NOPE + QK_ROPE
    pad_q = ((0, 0), (0, 0), (0, LANES - dq))
    wq3 = w_qb.reshape(Q_LORA, N_HEADS, dq)
    w_q_plain = jnp.pad(wq3, pad_q).reshape(Q_LORA, HEAD_W)
    w_q_rot = jnp.pad(_rot_cols(wq3[:, :, QK_NOPE:]), ((0, 0), (0, 0), (ROPE_LO, LANES - dq))).reshape(Q_LORA, HEAD_W)
    w_q = jnp.concatenate([w_q_plain, w_q_rot], axis=1)

    w_kv = w_kvb
    e_one = (jnp.arange(LANES) == 0).astype(f32)[None]

    wkv3 = w_kvb.reshape(KV_LORA, N_HEADS, QK_NOPE + V_DIM)
    wk_t = jnp.transpose(wkv3[:, :, :QK_NOPE], (1, 2, 0))
    wk_t = jnp.pad(wk_t, ((0, 0), (0, LANES - QK_NOPE), (0, 0)))
    sel = np.zeros((LANES, LANES), np.float32)
    sel[ROPE_LO + np.arange(QK_ROPE), np.arange(QK_ROPE)] = 1.0
    w_abs = jnp.concatenate([wk_t, jnp.broadcast_to(jnp.asarray(sel), (N_HEADS, LANES, LANES))], axis=2)
    wv = jnp.transpose(wkv3[:, :, QK_NOPE:], (1, 0, 2))
    w_vh = (wv[:, :, None, :] * jnp.eye(N_HEADS, dtype=f32)[:, None, :, None]).reshape(N_HEADS, KV_LORA, D_ATT)

    row = lambda v: v.reshape(1, -1)
    return {
        "g_pre": row(g_pre), "w_in_t": jnp.swapaxes(w_in, 0, 1).astype(bf16), "conv_w": conv_w,
        "conv_b": row(conv_b), "ln_g": row(ln_g), "ln_b": row(ln_b), "g_qa": row(g_qa), "w_q": w_q.astype(bf16),
        "g_kva": row(g_kva), "w_kv": w_kv.astype(bf16), "e_one": e_one,
        "w_abs": w_abs.astype(bf16), "w_vh": w_vh.astype(bf16),
        "w_out": w_out.astype(bf16), "g_post": row(g_post),
    }


def _rope_tables(start, length):
    inv = ROPE_THETA ** (-np.arange(0, QK_ROPE, 2, dtype=np.float64) / QK_ROPE)
    ang = np.arange(start, start + length, dtype=np.float64)[:, None] * inv[None, :]
    cos, sin = np.cos(ang), np.sin(ang)
    z = lambda n: np.zeros((length, n))
    tail = LANES - QK_NOPE - QK_ROPE
    cs_q = np.concatenate([np.full((length, QK_NOPE), Q_SCALE), cos * Q_SCALE, cos * Q_SCALE, z(tail)], axis=1)
    sn_q = np.concatenate([z(QK_NOPE), sin * Q_SCALE, sin * Q_SCALE, z(tail)], axis=1)
    cs_k = np.concatenate([cos, cos, z(LANES - QK_ROPE)], axis=1)
    sn_k = np.concatenate([-sin, sin, z(LANES - QK_ROPE)], axis=1)
    return tuple(jnp.asarray(t, dtype=f32) for t in (cs_q, sn_q, cs_k, sn_k))


def _layer(x_prompt, x_sample, ckv_past, kr_past, conv_state, w):
    B, S, _ = x_prompt.shape
    Bs, T, _ = x_sample.shape
    past = ckv_past.shape[1]
    pad_hist = lambda hst: jnp.pad(hst, ((0, 0), (HIST_PAD - HIST, 0), (0, 0)))

    qp, kp, vp, ycp, gap, ckvp, krp_t, nhp = _proj_call(
        x_prompt, jnp.zeros((B, HIST_PAD, D_CONV), f32), _rope_tables(0, S), w, 1, PROJ_ROWS, True)
    yap = _attend_prompt_call(qp, kp, vp, gap)
    y_prompt = _out_call(ycp.reshape(B * S, D_CONV), yap.reshape(B * S, D_ATT),
                         x_prompt.reshape(B * S, D_MODEL), w).reshape(B, S, D_MODEL)

    qs, ycs, gas, ckvs, krs, nhs = _proj_call(
        x_sample, pad_hist(conv_state), _rope_tables(past, T), w, SAMPLE_GROUP, T, False)
    yas = _attend_sample_call(qs, ckvs, krs, ckv_past, jnp.swapaxes(kr_past, 1, 2), w, gas)
    y_sample = _out_call(ycs.reshape(Bs * T, D_CONV), yas.reshape(Bs * T, D_ATT),
                         x_sample.reshape(Bs * T, D_MODEL), w).reshape(Bs, T, D_MODEL)

    trim = lambda nh: nh[:, HIST_PAD - HIST:, :]
    return y_prompt, y_sample, ckvp, jnp.swapaxes(krp_t, 1, 2), trim(nhp), ckvs, krs, trim(nhs)


def kernel(x_prompt, x_sample, cache_ckv, cache_krope, state_conv, g_pre, w_in, conv_w, conv_b, conv_ln_g,
           conv_ln_b, g_qa, w_qb, g_kva, w_kvb, w_out, g_post):
    depth = w_in.shape[0]
    yp, ys = x_prompt, x_sample
    outs = [[] for _ in range(6)]
    for l in range(depth):
        w = _layout_weights(g_pre[l], w_in[l], conv_w[l], conv_b[l], conv_ln_g[l], conv_ln_b[l], g_qa[l],
                            w_qb[l], g_kva[l], w_kvb[l], w_out[l], g_post[l])
        yp, ys, *caches = _layer(yp, ys, cache_ckv[l], cache_krope[l], state_conv[l], w)
        for dst, val in zip(outs, caches):
            dst.append(val)
    return (yp, ys) + tuple(jnp.stack(o) for o in outs)
```
